```python
import math
import jax, jax.numpy as jnp
from jax import lax
import numpy as np

D_MODEL = 2048
BATCH = 16
SEQ = 2048
DEPTH = 2

CHUNK = 64
Q_BLOCK = 128
N_BRANCH = 4
BRANCH_WIDTH = D_MODEL // N_BRANCH
GMLP_BLOCK = 128
A_GROUP_W = 128
A_GROUPS = BRANCH_WIDTH // A_GROUP_W
B_HEAD_DIM = 64
B_HEADS = BRANCH_WIDTH // B_HEAD_DIM
C_KDIM = 128
C_VDIM = 128
C_HEADS = BRANCH_WIDTH // C_VDIM
C_FDIM = C_HEADS * C_KDIM
D_HEAD = 64
D_HEADS = BRANCH_WIDTH // (2 * D_HEAD)
D_FF = 5632
CONV_W = 3
ROPE_THETA = 10000.0
EPS = 1e-6

A_COLS = 2 * BRANCH_WIDTH
B_COLS = 3 * BRANCH_WIDTH + B_HEADS
C_COLS = 2 * C_FDIM + 2 * BRANCH_WIDTH
D_COLS = 3 * BRANCH_WIDTH
IN_COLS = A_COLS + B_COLS + C_COLS + D_COLS
IN_SPLITS = [A_COLS, A_COLS + B_COLS, A_COLS + B_COLS + C_COLS]

kernel_name = 'hybrid_chunk_causal_encoder'


def rms_norm(x, g):
    xf = x.astype(jnp.float32)
    y = xf * lax.rsqrt(jnp.mean(xf * xf, axis=-1, keepdims=True) + EPS)
    return (y * g.astype(jnp.float32)).astype(x.dtype)


def layer_norm(x, g, b):
    xf = x.astype(jnp.float32)
    mu = jnp.mean(xf, axis=-1, keepdims=True)
    var = jnp.mean(jnp.square(xf - mu), axis=-1, keepdims=True)
    y = (xf - mu) * lax.rsqrt(var + EPS) * g.astype(jnp.float32) + b.astype(jnp.float32)
    return y.astype(x.dtype)


def split_heads(t, n_heads):
    b, s, _ = t.shape
    return t.reshape(b, s, n_heads, -1).transpose(0, 2, 1, 3)


def merge_heads(t):
    b, h, s, d = t.shape
    return t.transpose(0, 2, 1, 3).reshape(b, s, h * d)


def rope(t, positions):
    half = t.shape[-1] // 2
    inv_freq = ROPE_THETA ** (-jnp.arange(half, dtype=jnp.float32) / half)
    ang = positions.astype(jnp.float32)[:, None] * inv_freq[None, :]
    cos, sin = jnp.cos(ang), jnp.sin(ang)
    tf = t.astype(jnp.float32)
    t1, t2 = tf[..., :half], tf[..., half:]
    return jnp.concatenate([t1 * cos - t2 * sin, t1 * sin + t2 * cos], axis=-1).astype(t.dtype)


def masked_softmax(logits, mask):
    return jax.nn.softmax(jnp.where(mask, logits, -jnp.inf), axis=-1)


def gmlp_spatial_gating(u, v, ln_g, ln_b, w_s, b_s):
    b, s, _ = v.shape
    v = layer_norm(v, ln_g, ln_b)
    vb = v.reshape(b, s // GMLP_BLOCK, GMLP_BLOCK, A_GROUPS, A_GROUP_W)
    cid = jnp.arange(GMLP_BLOCK) // CHUNK
    mask = cid[:, None] >= cid[None, :]
    w = jnp.where(mask[None], w_s, jnp.zeros((), w_s.dtype))
    mixed = jnp.einsum('gts,bnsgc->bntgc', w, vb) + b_s.T[:, :, None]
    return u * mixed.reshape(b, s, A_GROUPS * A_GROUP_W)


def forgetting_attention(q, k, v, log_f):
    s = q.shape[2]
    scale = q.shape[-1] ** -0.5
    c = jnp.cumsum(log_f, axis=-1)
    pos = jnp.arange(s)
    outs = []
    for i in range(s // Q_BLOCK):
        q0, qe = i * Q_BLOCK, (i + 1) * Q_BLOCK
        logits = (jnp.einsum('bhqd,bhkd->bhqk', q[:, :, q0:qe], k[:, :, :qe]).astype(jnp.float32) * scale
                  + c[:, :, q0:qe, None] - c[:, :, None, :qe])
        mask = pos[q0:qe, None] >= pos[None, :qe]
        p = masked_softmax(logits, mask)
        outs.append(jnp.einsum('bhqk,bhkd->bhqd', p.astype(v.dtype), v[:, :, :qe]))
    return jnp.concatenate(outs, axis=2)


def differential_attention(q1, q2, k1, k2, v, lam):
    s = q1.shape[2]
    scale = q1.shape[-1] ** -0.5
    cid = jnp.arange(s) // CHUNK
    outs = []
    for i in range(s // Q_BLOCK):
        q0, qe = i * Q_BLOCK, (i + 1) * Q_BLOCK
        mask = cid[q0:qe, None] >= cid[None, :qe]
        l1 = jnp.einsum('bhqd,bhkd->bhqk', q1[:, :, q0:qe], k1[:, :, :qe]).astype(jnp.float32) * scale
        l2 = jnp.einsum('bhqd,bhkd->bhqk', q2[:, :, q0:qe], k2[:, :, :qe]).astype(jnp.float32) * scale
        p = masked_softmax(l1, mask) - lam * masked_softmax(l2, mask)
        outs.append(jnp.einsum('bhqk,bhkd->bhqd', p.astype(v.dtype), v[:, :, :qe]))
    return jnp.concatenate(outs, axis=2)


def hgrn2_scan(q, k, v, log_f):
    b, h, s, dk = q.shape
    dv = v.shape[-1]
    n = s // CHUNK

    def chunks(t):
        return t.reshape(b, h, n, CHUNK, t.shape[-1]).transpose(2, 0, 1, 3, 4)

    tri = jnp.tril(jnp.ones((CHUNK, CHUNK), dtype=bool))[:, :, None]

    def step(state, inp):
        qc, kc, vc, lfc = inp
        qc = qc.astype(jnp.float32)
        kc = kc.astype(jnp.float32)
        vc = vc.astype(jnp.float32)
        cum = jnp.cumsum(lfc.astype(jnp.float32), axis=2)
        decay = jnp.exp(jnp.where(tri, cum[:, :, :, None, :] - cum[:, :, None, :, :], -jnp.inf))
        scores = jnp.einsum('bhtk,bhsk,bhtsk->bhts', qc, kc, decay)
        out = (jnp.einsum('bhts,bhsv->bhtv', scores, vc)
               + jnp.einsum('bhtk,bhkv->bhtv', qc * jnp.exp(cum), state))
        last = cum[:, :, -1:, :]
        state = (jnp.exp(last[:, :, 0, :])[..., None] * state
                 + jnp.einsum('bhsk,bhsv->bhkv', kc * jnp.exp(last - cum), vc))
        return state, out

    state0 = jnp.zeros((b, h, dk, dv), jnp.float32)
    _, outs = lax.scan(step, state0, (chunks(q), chunks(k), chunks(v), chunks(log_f)))
    return outs.transpose(1, 2, 0, 3, 4).reshape(b, h, s, dv)


def conv_ffn(x, w_up, conv_w, conv_b, w_down):
    h = x @ w_up
    h = lax.conv_general_dilated(h, conv_w[:, None, :], window_strides=(1,), padding=[(CONV_W - 1, 0)],
                                 dimension_numbers=('NWC', 'WIO', 'NWC'),
                                 feature_group_count=h.shape[-1]) + conv_b
    a, g = jnp.split(h, 2, axis=-1)
    return (jax.nn.gelu(a) * g) @ w_down


def setup_inputs(seed: int = 0) -> dict:
    key = jax.random.key(seed)
    ks = jax.random.split(key, 22)

    def nrm(k, shape, scale):
        return jax.random.normal(k, shape, jnp.float32) * scale

    def gain(k, shape):
        return 1.0 + nrm(k, shape, 0.02)

    return {
        'x': nrm(ks[0], (BATCH, SEQ, D_MODEL), 1.0),
        'norm_mix_g': gain(ks[1], (DEPTH, D_MODEL)),
        'w_in': nrm(ks[2], (DEPTH, D_MODEL, IN_COLS), D_MODEL ** -0.5),
        'fox_b_f': nrm(ks[3], (DEPTH, B_HEADS), 0.1),
        'gmlp_ln_g': gain(ks[4], (DEPTH, BRANCH_WIDTH)),
        'gmlp_ln_b': nrm(ks[5], (DEPTH, BRANCH_WIDTH), 0.02),
        'gmlp_w_s': nrm(ks[6], (DEPTH, A_GROUPS, GMLP_BLOCK, GMLP_BLOCK), GMLP_BLOCK ** -0.5),
        'gmlp_b_s': 1.0 + nrm(ks[7], (DEPTH, A_GROUPS, GMLP_BLOCK), 0.02),
        'hgrn_lb_logits': nrm(ks[8], (DEPTH, C_FDIM), 0.1),
        'hgrn_norm_g': gain(ks[9], (DEPTH, C_VDIM)),
        'diff_lambda': nrm(ks[10], (DEPTH, 4, D_HEAD), 0.1),
        'diff_norm_g': gain(ks[11], (DEPTH, 2 * D_HEAD)),
        'w_branch': nrm(ks[12], (DEPTH, N_BRANCH, BRANCH_WIDTH, D_MODEL), BRANCH_WIDTH ** -0.5),
        'w_gate': nrm(ks[13], (DEPTH, N_BRANCH, D_MODEL, D_MODEL), D_MODEL ** -0.5),
        'b_gate': nrm(ks[14], (DEPTH, N_BRANCH, D_MODEL), 0.1),
        'w_out': nrm(ks[15], (DEPTH, D_MODEL, D_MODEL), D_MODEL ** -0.5),
        'norm_ffn_g': gain(ks[16], (DEPTH, D_MODEL)),
        'ffn_w_up': nrm(ks[17], (DEPTH, D_MODEL, 2 * D_FF), D_MODEL ** -0.5),
        'ffn_conv_w': nrm(ks[18], (DEPTH, CONV_W, 2 * D_FF), CONV_W ** -0.5),
        'ffn_conv_b': nrm(ks[19], (DEPTH, 2 * D_FF), 0.02),
        'ffn_w_down': nrm(ks[20], (DEPTH, D_FF, D_MODEL), D_FF ** -0.5),
        'norm_final_g': gain(ks[21], (D_MODEL,)),
    }


def reference(x, norm_mix_g, w_in, fox_b_f, gmlp_ln_g, gmlp_ln_b, gmlp_w_s, gmlp_b_s,
              hgrn_lb_logits, hgrn_norm_g, diff_lambda, diff_norm_g, w_branch, w_gate, b_gate,
              w_out, norm_ffn_g, ffn_w_up, ffn_conv_w, ffn_conv_b, ffn_w_down, norm_final_g):
    b, s, _ = x.shape
    dt = x.dtype
    positions = jnp.arange(s, dtype=jnp.int32)
    lb_p = jax.nn.softmax(hgrn_lb_logits.astype(jnp.float32), axis=0)
    lower_bounds = jnp.cumsum(lb_p, axis=0) - lb_p[0]

    for l in range(DEPTH):
        xn = rms_norm(x, norm_mix_g[l])
        pa, pb, pc, pd = jnp.split(xn @ w_in[l], IN_SPLITS, axis=-1)

        ua, va = jnp.split(jax.nn.gelu(pa), 2, axis=-1)
        y_a = gmlp_spatial_gating(ua, va, gmlp_ln_g[l], gmlp_ln_b[l], gmlp_w_s[l], gmlp_b_s[l])

        bq, bk, bv, bz = jnp.split(pb, [BRANCH_WIDTH, 2 * BRANCH_WIDTH, 3 * BRANCH_WIDTH], axis=-1)
        fox_log_f = jax.nn.log_sigmoid((bz + fox_b_f[l]).astype(jnp.float32)).transpose(0, 2, 1)
        y_b = merge_heads(forgetting_attention(split_heads(bq, B_HEADS), split_heads(bk, B_HEADS),
                                               split_heads(bv, B_HEADS), fox_log_f))

        cq, cz, ci, cg = jnp.split(pc, [C_FDIM, 2 * C_FDIM, 2 * C_FDIM + BRANCH_WIDTH], axis=-1)
        lb = lower_bounds[l]
        cz32 = cz.astype(jnp.float32)
        c_f = lb + (1.0 - lb) * jax.nn.sigmoid(cz32)
        c_log_f = jnp.log(c_f)
        c_k = (1.0 - lb) * jax.nn.sigmoid(-cz32)
        o_c = hgrn2_scan(split_heads(cq * C_KDIM ** -0.5, C_HEADS), split_heads(c_k, C_HEADS),
                         split_heads(ci, C_HEADS), split_heads(c_log_f, C_HEADS)).astype(dt)
        y_c = merge_heads(rms_norm(o_c, hgrn_norm_g[l])) * jax.nn.sigmoid(cg)

        dq, dk, dv = jnp.split(pd, 3, axis=-1)
        dq = dq.reshape(b, s, D_HEADS, 2, D_HEAD).transpose(3, 0, 2, 1, 4)
        dk = dk.reshape(b, s, D_HEADS, 2, D_HEAD).transpose(3, 0, 2, 1, 4)
        lam_init = 0.8 - 0.6 * math.exp(-0.3 * l)
        lam_p = diff_lambda[l].astype(jnp.float32)
        lam = jnp.exp(jnp.sum(lam_p[0] * lam_p[1])) - jnp.exp(jnp.sum(lam_p[2] * lam_p[3])) + lam_init
        o_d = differential_attention(rope(dq[0], positions), rope(dq[1], positions),
                                     rope(dk[0], positions), rope(dk[1], positions),
                                     split_heads(dv, D_HEADS), lam)
        y_d = merge_heads(rms_norm(o_d, diff_norm_g[l]) * (1.0 - lam_init))

        mixed = jnp.zeros_like(x)
        for n, y_br in enumerate((y_a, y_b, y_c, y_d)):
            gate = jax.nn.sigmoid(xn @ w_gate[l, n] + b_gate[l, n])
            mixed = mixed + gate * (y_br @ w_branch[l, n])
        x = x + mixed @ w_out[l]

        x = x + conv_ffn(rms_norm(x, norm_ffn_g[l]), ffn_w_up[l], ffn_conv_w[l], ffn_conv_b[l], ffn_w_down[l])

    return rms_norm(x, norm_final_g)
```

```python
import functools
import math

import jax
import jax.numpy as jnp
from jax import lax
from jax.experimental import pallas as pl
from jax.experimental.pallas import tpu as pltpu

F32 = jnp.float32
BF16 = jnp.bfloat16

D_MODEL = 2048
DEPTH = 2
CHUNK = 64
BRANCH_WIDTH = 512
N_BRANCH = 4
GMLP_BLOCK = 128
A_GROUPS = 4
B_HEADS = 8
C_KDIM = 128
D_HEAD = 64
D_FF = 5632
ROPE_THETA = 10000.0
EPS = 1e-6
LANES = 128

IN_COLS_PAD = 6400
COL_A = 0
COL_BQ, COL_BK, COL_BV, COL_BZ = 8, 12, 16, 20
COL_CQ, COL_CZ, COL_CI, COL_CG = 21, 25, 29, 33
COL_DQ, COL_DK, COL_DV = 37, 41, 45

VMEM_LIMIT = 56 * 1024 * 1024


def _cparams(*sem):
    return pltpu.CompilerParams(dimension_semantics=sem, vmem_limit_bytes=VMEM_LIMIT)


def _rms(x, g):
    ms = jnp.mean(x * x, axis=-1, keepdims=True)
    return x * lax.rsqrt(ms + EPS) * g


def _dot(a, b):
    return jnp.dot(a, b, preferred_element_type=F32)


def _dot_nt(a, b):
    return lax.dot_general(a, b, (((1,), (1,)), ((), ())), preferred_element_type=F32)


def _dot_tn(a, b):
    return lax.dot_general(a, b, (((0,), (0,)), ((), ())), preferred_element_type=F32)


def _tri_cumsum(tri, x):
    hi = x.astype(BF16)
    r1 = x - hi.astype(F32)
    mid = r1.astype(BF16)
    lo = (r1 - mid.astype(F32)).astype(BF16)
    return (_dot(tri, hi) + _dot(tri, mid)) + _dot(tri, lo)


def _lower_tri(n):
    return (lax.broadcasted_iota(jnp.int32, (n, n), 0) >= lax.broadcasted_iota(jnp.int32, (n, n), 1)).astype(BF16)


def _rms_inproj_kernel(x_ref, g_ref, w_ref, p_ref, xn_ref):
    @pl.when(pl.program_id(1) == 0)
    def _():
        xn_ref[...] = _rms(x_ref[...], g_ref[...]).astype(BF16)

    p_ref[...] = _dot(xn_ref[...], w_ref[...])


def _rms_inproj(x2, g, w, tm, tn):
    t, np_ = x2.shape[0], w.shape[1]
    return pl.pallas_call(
        _rms_inproj_kernel,
        grid=(t // tm, np_ // tn),
        in_specs=[pl.BlockSpec((tm, D_MODEL), lambda i, j: (i, 0)),
                  pl.BlockSpec((1, D_MODEL), lambda i, j: (0, 0)),
                  pl.BlockSpec((D_MODEL, tn), lambda i, j: (0, j))],
        out_specs=[pl.BlockSpec((tm, tn), lambda i, j: (i, j)),
                   pl.BlockSpec((tm, D_MODEL), lambda i, j: (i, 0))],
        out_shape=[jax.ShapeDtypeStruct((t, np_), F32), jax.ShapeDtypeStruct((t, D_MODEL), BF16)],
        compiler_params=_cparams("parallel", "arbitrary"),
        name="rms_inproj",
    )(x2, g, w)


def _gmlp_kernel(p_ref, lng_ref, lnb_ref, ws_ref, bs_ref, y_ref):
    ta = p_ref.shape[0]
    pa = jax.nn.gelu(p_ref[...])
    u = pa[:, :BRANCH_WIDTH]
    v = pa[:, BRANCH_WIDTH:]
    mu = jnp.mean(v, axis=-1, keepdims=True)
    d = v - mu
    var = jnp.mean(d * d, axis=-1, keepdims=True)
    vn = (d * lax.rsqrt(var + EPS) * lng_ref[...] + lnb_ref[...]).astype(BF16)
    rc = lax.broadcasted_iota(jnp.int32, (GMLP_BLOCK, GMLP_BLOCK), 0) // CHUNK
    cc = lax.broadcasted_iota(jnp.int32, (GMLP_BLOCK, GMLP_BLOCK), 1) // CHUNK
    mask = rc >= cc
    for g in range(A_GROUPS):
        wg = jnp.where(mask, ws_ref[g], 0.0).astype(BF16)
        bcol = bs_ref[:, g:g + 1]
        cs = slice(g * LANES, (g + 1) * LANES)
        for blk in range(ta // GMLP_BLOCK):
            rs = slice(blk * GMLP_BLOCK, (blk + 1) * GMLP_BLOCK)
            mixed = _dot(wg, vn[rs, cs]) + bcol
            y_ref[rs, cs] = (u[rs, cs] * mixed).astype(BF16)


def _gmlp(p, ln_g, ln_b, w_s, b_s_t, ta):
    t = p.shape[0]
    return pl.pallas_call(
        _gmlp_kernel,
        grid=(t // ta,),
        in_specs=[pl.BlockSpec((ta, 2 * BRANCH_WIDTH), lambda i: (i, 0)),
                  pl.BlockSpec((1, BRANCH_WIDTH), lambda i: (0, 0)),
                  pl.BlockSpec((1, BRANCH_WIDTH), lambda i: (0, 0)),
                  pl.BlockSpec((A_GROUPS, GMLP_BLOCK, GMLP_BLOCK), lambda i: (0, 0, 0)),
                  pl.BlockSpec((GMLP_BLOCK, A_GROUPS), lambda i: (0, 0))],
        out_specs=pl.BlockSpec((ta, BRANCH_WIDTH), lambda i: (i, 0)),
        out_shape=jax.ShapeDtypeStruct((t, BRANCH_WIDTH), BF16),
        compiler_params=_cparams("parallel"),
        name="gmlp",
    )(p, ln_g, ln_b, w_s, b_s_t)


def _foxgate_kernel(p_ref, bf_ref, c_ref, ct_ref, *, tb):
    s = p_ref.shape[0]
    seg = 256 if s % 256 == 0 else s
    z = p_ref[...] + bf_ref[...]
    ls = jnp.minimum(z, 0.0) - jnp.log1p(jnp.exp(-jnp.abs(z)))
    tri = _lower_tri(seg)
    carry = jnp.zeros((1, LANES), F32)
    for blk in range(s // seg):
        rs = slice(blk * seg, (blk + 1) * seg)
        cs = _tri_cumsum(tri, ls[rs]) + carry
        c_ref[rs, :] = cs
        carry = cs[seg - 1:seg, :]
    ct = c_ref[...].T
    for j in range(s // tb):
        ct_ref[0, j] = ct[:B_HEADS, j * tb:(j + 1) * tb]


def _foxgate(p, bf, b, s, tb):
    return pl.pallas_call(
        functools.partial(_foxgate_kernel, tb=tb),
        grid=(b,),
        in_specs=[pl.BlockSpec((s, LANES), lambda i: (i, COL_BZ)),
                  pl.BlockSpec((1, LANES), lambda i: (0, 0))],
        out_specs=[pl.BlockSpec((s, LANES), lambda i: (i, 0)),
                   pl.BlockSpec((1, s // tb, B_HEADS, tb), lambda i: (i, 0, 0, 0))],
        out_shape=[jax.ShapeDtypeStruct((b * s, LANES), F32),
                   jax.ShapeDtypeStruct((b, s // tb, B_HEADS, tb), F32)],
        compiler_params=_cparams("parallel"),
        name="foxgate",
    )(p, bf)


def _flash_rows(qh, kt_ref, vb_ref, qi, tb, bias_fn, diag_mask):
    def step(j, carry, masked):
        m, l, acc = carry
        sc = _dot(qh, kt_ref[j])
        if bias_fn is not None:
            sc = bias_fn(sc, j)
        if masked:
            sc = jnp.where(diag_mask, sc, -jnp.inf)
        m_new = jnp.maximum(m, jnp.max(sc, axis=-1, keepdims=True))
        alpha = jnp.exp(m - m_new)
        pr = jnp.exp(sc - m_new)
        l = alpha * l + jnp.sum(pr, axis=-1, keepdims=True)
        vj = vb_ref[pl.ds(pl.multiple_of(j * tb, tb), tb), :]
        acc = alpha * acc + _dot(pr.astype(BF16), vj)
        return m_new, l, acc

    init = (jnp.full((tb, 1), -jnp.inf, F32), jnp.zeros((tb, 1), F32), jnp.zeros((tb, LANES), F32))
    carry = lax.fori_loop(0, qi, lambda j, c: step(j, c, False), init)
    _, l, acc = step(qi, carry, True)
    return acc / l


def _fox_kernel(q_ref, k_ref, v_ref, c_ref, ct_ref, o_ref, kt_s, vb_s, *, tb):
    s = q_ref.shape[0]
    nb = s // tb
    hp = pl.program_id(1)
    for j in range(nb):
        kt_s[j] = k_ref[j * tb:(j + 1) * tb, :].T.astype(BF16)
    vb_s[...] = v_ref[...].astype(BF16)
    lane = lax.broadcasted_iota(jnp.int32, (1, LANES), 1)
    causal = lax.broadcasted_iota(jnp.int32, (tb, tb), 0) >= lax.broadcasted_iota(jnp.int32, (tb, tb), 1)
    scale = 64 ** -0.5

    def qblock(qi, _):
        r0 = pl.multiple_of(qi * tb, tb)
        q = q_ref[pl.ds(r0, tb), :] * scale
        cblk = c_ref[pl.ds(r0, tb), :]
        outs = []
        for hh in range(2):
            h = 2 * hp + hh
            qh = jnp.where((lane // 64) == hh, q, 0.0).astype(BF16)
            ccol = jnp.sum(jnp.where(lane == h, cblk, 0.0), axis=-1, keepdims=True)

            def bias(sc, j, h=h, ccol=ccol):
                return sc + (ccol - ct_ref[0, j, pl.ds(h, 1), :])

            outs.append(_flash_rows(qh, kt_s, vb_s, qi, tb, bias, causal))
        o_ref[pl.ds(r0, tb), :] = jnp.where(lane < 64, outs[0], outs[1]).astype(BF16)
        return 0

    lax.fori_loop(0, nb, qblock, 0)


def _fox(p, c, ct, b, s, tb):
    nb = s // tb
    return pl.pallas_call(
        functools.partial(_fox_kernel, tb=tb),
        grid=(b, B_HEADS // 2),
        in_specs=[pl.BlockSpec((s, LANES), lambda i, h: (i, COL_BQ + h)),
                  pl.BlockSpec((s, LANES), lambda i, h: (i, COL_BK + h)),
                  pl.BlockSpec((s, LANES), lambda i, h: (i, COL_BV + h)),
                  pl.BlockSpec((s, LANES), lambda i, h: (i, 0)),
                  pl.BlockSpec((1, nb, B_HEADS, tb), lambda i, h: (i, 0, 0, 0))],
        out_specs=pl.BlockSpec((s, LANES), lambda i, h: (i, h)),
        out_shape=jax.ShapeDtypeStruct((b * s, BRANCH_WIDTH), BF16),
        scratch_shapes=[pltpu.VMEM((nb, LANES, tb), BF16), pltpu.VMEM((s, LANES), BF16)],
        compiler_params=_cparams("parallel", "arbitrary"),
        name="fox_attn",
    )(p, p, p, c, ct)


def _diff_kernel(q_ref, k_ref, v_ref, cos_ref, sin_ref, lam_ref, g_ref, o_ref, kt_s, vb_s, *, tb, lam_init):
    s = q_ref.shape[0]
    nb = s // tb
    lane = lax.broadcasted_iota(jnp.int32, (1, LANES), 1)
    lower_half = (lane % D_HEAD) < (D_HEAD // 2)

    def rope(x, cos, sin_signed):
        partner = jnp.where(lower_half, pltpu.roll(x, LANES - D_HEAD // 2, 1), pltpu.roll(x, D_HEAD // 2, 1))
        return x * cos + partner * sin_signed

    for j in range(nb):
        rs = slice(j * tb, (j + 1) * tb)
        kt_s[j] = rope(k_ref[rs, :], cos_ref[rs, :], sin_ref[rs, :]).T.astype(BF16)
    vb_s[...] = v_ref[...].astype(BF16)

    lp = lam_ref[...]
    lam = (jnp.exp(jnp.sum(lp[0:1] * lp[1:2], axis=-1, keepdims=True))
           - jnp.exp(jnp.sum(lp[2:3] * lp[3:4], axis=-1, keepdims=True)) + lam_init)
    chunk_mask = (lax.broadcasted_iota(jnp.int32, (tb, tb), 0) // CHUNK
                  >= lax.broadcasted_iota(jnp.int32, (tb, tb), 1) // CHUNK)
    scale = D_HEAD ** -0.5

    def qblock(qi, _):
        r0 = pl.multiple_of(qi * tb, tb)
        rs = pl.ds(r0, tb)
        q = rope(q_ref[rs, :], cos_ref[rs, :], sin_ref[rs, :]) * scale
        q1 = jnp.where(lane < D_HEAD, q, 0.0).astype(BF16)
        q2 = jnp.where(lane >= D_HEAD, q, 0.0).astype(BF16)
        o1 = _flash_rows(q1, kt_s, vb_s, qi, tb, None, chunk_mask)
        o2 = _flash_rows(q2, kt_s, vb_s, qi, tb, None, chunk_mask)
        o = o1 - lam * o2
        o_ref[rs, :] = (_rms(o, g_ref[...]) * (1.0 - lam_init)).astype(BF16)
        return 0

    lax.fori_loop(0, nb, qblock, 0)


def _diff(p, cos, sin_signed, lam_p, norm_g, b, s, tb, lam_init):
    nb = s // tb
    return pl.pallas_call(
        functools.partial(_diff_kernel, tb=tb, lam_init=lam_init),
        grid=(b, BRANCH_WIDTH // LANES),
        in_specs=[pl.BlockSpec((s, LANES), lambda i, h: (i, COL_DQ + h)),
                  pl.BlockSpec((s, LANES), lambda i, h: (i, COL_DK + h)),
                  pl.BlockSpec((s, LANES), lambda i, h: (i, COL_DV + h)),
                  pl.BlockSpec((s, LANES), lambda i, h: (0, 0)),
                  pl.BlockSpec((s, LANES), lambda i, h: (0, 0)),
                  pl.BlockSpec((4, D_HEAD), lambda i, h: (0, 0)),
                  pl.BlockSpec((1, LANES), lambda i, h: (0, 0))],
        out_specs=pl.BlockSpec((s, LANES), lambda i, h: (i, h)),
        out_shape=jax.ShapeDtypeStruct((b * s, BRANCH_WIDTH), BF16),
        scratch_shapes=[pltpu.VMEM((nb, LANES, tb), BF16), pltpu.VMEM((s, LANES), BF16)],
        compiler_params=_cparams("parallel", "arbitrary"),
        name="diff_attn",
    )(p, p, p, cos, sin_signed, lam_p, norm_g)


SUB = 16


def _hgrn_kernel(q_ref, z_ref, i_ref, g_ref, lbl_ref, ng_ref, o_ref, *, layer):
    s = q_ref.shape[0]
    lbl = lbl_ref[...]
    e = jnp.exp(lbl - jnp.max(lbl, axis=0, keepdims=True))
    pr = e / jnp.sum(e, axis=0, keepdims=True)
    lb = pr[0:1] - pr[0:1]
    for r in range(1, layer + 1):
        lb = lb + pr[r:r + 1]
    oml = 1.0 - lb
    tri = _lower_tri(CHUNK)
    lane_s = lax.broadcasted_iota(jnp.int32, (SUB, CHUNK), 1)
    row_s = lax.broadcasted_iota(jnp.int32, (SUB, CHUNK), 0)
    qscale = C_KDIM ** -0.5

    def chunk_body(n, st):
        rs = pl.ds(pl.multiple_of(n * CHUNK, CHUNK), CHUNK)
        z = z_ref[rs, :]
        f = lb + oml * jax.nn.sigmoid(z)
        kk = oml * jax.nn.sigmoid(-z)
        cum = _tri_cumsum(tri, jnp.log(f))
        q = q_ref[rs, :] * qscale
        vb = i_ref[rs, :].astype(BF16)
        inter = _dot_nt((q * jnp.exp(cum)).astype(BF16), st.astype(BF16))
        blocks = []
        for bi in range(CHUNK // SUB):
            bs = slice(bi * SUB, (bi + 1) * SUB)
            qb, cb, kb = q[bs], cum[bs], kk[bs]
            if bi == 0:
                sc = jnp.zeros((SUB, CHUNK), F32)
            else:
                ref_row = cum[bi * SUB - 1:bi * SUB]
                qi_ = (qb * jnp.exp(cb - ref_row)).astype(BF16)
                ki_ = (kk * jnp.exp(jnp.minimum(ref_row - cum, 0.0))).astype(BF16)
                sc = jnp.where(lane_s < bi * SUB, _dot_nt(qi_, ki_), 0.0)
            for si in range(SUB):
                x = qb * jnp.exp(jnp.minimum(cb - cb[si:si + 1], 0.0)) * kb[si:si + 1]
                col = jnp.sum(x, axis=-1, keepdims=True)
                sc = jnp.where((lane_s == bi * SUB + si) & (row_s >= si), col, sc)
            blocks.append(sc)
        scores = jnp.concatenate(blocks, axis=0)
        out = inter + _dot(scores.astype(BF16), vb)
        last = cum[CHUNK - 1:CHUNK]
        kdec = (kk * jnp.exp(last - cum)).astype(BF16)
        st_new = st * jnp.exp(last) + _dot_tn(vb, kdec)
        y = _rms(out, ng_ref[...]) * jax.nn.sigmoid(g_ref[rs, :])
        o_ref[rs, :] = y.astype(BF16)
        return st_new

    lax.fori_loop(0, s // CHUNK, chunk_body, jnp.zeros((LANES, C_KDIM), F32))


def _hgrn(p, lb_logits, norm_g, b, s, layer):
    return pl.pallas_call(
        functools.partial(_hgrn_kernel, layer=layer),
        grid=(b, BRANCH_WIDTH // LANES),
        in_specs=[pl.BlockSpec((s, LANES), lambda i, h: (i, COL_CQ + h)),
                  pl.BlockSpec((s, LANES), lambda i, h: (i, COL_CZ + h)),
                  pl.BlockSpec((s, LANES), lambda i, h: (i, COL_CI + h)),
                  pl.BlockSpec((s, LANES), lambda i, h: (i, COL_CG + h)),
                  pl.BlockSpec((DEPTH, LANES), lambda i, h: (0, h)),
                  pl.BlockSpec((1, LANES), lambda i, h: (0, 0))],
        out_specs=pl.BlockSpec((s, LANES), lambda i, h: (i, h)),
        out_shape=jax.ShapeDtypeStruct((b * s, BRANCH_WIDTH), BF16),
        compiler_params=_cparams("parallel", "arbitrary"),
        name="hgrn",
    )(p, p, p, p, lb_logits, norm_g)


def _merge_kernel(xn_ref, ya_ref, yb_ref, yc_ref, yd_ref, wg_ref, bg_ref, wb_ref, o_ref):
    xn = xn_ref[...]
    acc = None
    for n, y_ref in enumerate((ya_ref, yb_ref, yc_ref, yd_ref)):
        gate = jax.nn.sigmoid(_dot(xn, wg_ref[n]) + bg_ref[n:n + 1, :])
        term = gate * _dot(y_ref[...], wb_ref[n])
        acc = term if acc is None else acc + term
    o_ref[...] = acc.astype(BF16)


def _merge(xn, ys, wg, bg, wb, tm, tn):
    t = xn.shape[0]
    yspec = pl.BlockSpec((tm, BRANCH_WIDTH), lambda i, j: (i, 0))
    return pl.pallas_call(
        _merge_kernel,
        grid=(t // tm, D_MODEL // tn),
        in_specs=[pl.BlockSpec((tm, D_MODEL), lambda i, j: (i, 0)), yspec, yspec, yspec, yspec,
                  pl.BlockSpec((N_BRANCH, D_MODEL, tn), lambda i, j: (0, 0, j)),
                  pl.BlockSpec((N_BRANCH, tn), lambda i, j: (0, j)),
                  pl.BlockSpec((N_BRANCH, BRANCH_WIDTH, tn), lambda i, j: (0, 0, j))],
        out_specs=pl.BlockSpec((tm, tn), lambda i, j: (i, j)),
        out_shape=jax.ShapeDtypeStruct((t, D_MODEL), BF16),
        compiler_params=_cparams("parallel", "arbitrary"),
        name="merge",
    )(xn, *ys, wg, bg, wb)


def _outproj_kernel(x_ref, m_ref, w_ref, o_ref):
    o_ref[...] = x_ref[...] + _dot(m_ref[...], w_ref[...])


def _outproj(x2, mixed, w, tm, tn):
    t = x2.shape[0]
    return pl.pallas_call(
        _outproj_kernel,
        grid=(t // tm, D_MODEL // tn),
        in_specs=[pl.BlockSpec((tm, tn), lambda i, j: (i, j)),
                  pl.BlockSpec((tm, D_MODEL), lambda i, j: (i, 0)),
                  pl.BlockSpec((D_MODEL, tn), lambda i, j: (0, j))],
        out_specs=pl.BlockSpec((tm, tn), lambda i, j: (i, j)),
        out_shape=jax.ShapeDtypeStruct((t, D_MODEL), F32),
        compiler_params=_cparams("parallel", "arbitrary"),
        name="outproj",
    )(x2, mixed, w)


CARRY_ROWS = 8


def _ffn_kernel(x_ref, g_ref, wa_ref, wg_ref, cwa_ref, cwg_ref, cba_ref, cbg_ref, wd_ref, gf_ref, o_ref,
                xn_s, acc_s, carry_s, *, tiles_per_seq, final_norm):
    i = pl.program_id(0)
    j = pl.program_id(1)
    tm, tf = xn_s.shape[0], wa_ref.shape[1]

    @pl.when((i == 0) & (j == 0))
    def _():
        carry_s[...] = jnp.zeros_like(carry_s)

    @pl.when(j == 0)
    def _():
        xn_s[...] = _rms(x_ref[...], g_ref[...]).astype(BF16)
        acc_s[...] = jnp.zeros_like(acc_s)

    xn = xn_s[...]
    seq_start = (i % tiles_per_seq) == 0
    r8 = lax.broadcasted_iota(jnp.int32, (CARRY_ROWS, tf), 0)

    def conv(h, cw_ref, cb_ref, slot):
        prev = jnp.where(seq_start, 0.0, carry_s[slot])
        carry_s[slot] = h[tm - CARRY_ROWS:tm]
        h1 = pltpu.roll(h, 1, 0)
        h2 = pltpu.roll(h, 2, 0)
        p1 = pltpu.roll(prev, 1, 0)
        p2 = pltpu.roll(prev, 2, 0)
        h1 = jnp.concatenate([jnp.where(r8 < 1, p1, h1[:CARRY_ROWS]), h1[CARRY_ROWS:]], axis=0)
        h2 = jnp.concatenate([jnp.where(r8 < 2, p2, h2[:CARRY_ROWS]), h2[CARRY_ROWS:]], axis=0)
        cw = cw_ref[...]
        return cw[0:1] * h2 + cw[1:2] * h1 + cw[2:3] * h + cb_ref[...]

    ha = conv(_dot(xn, wa_ref[...]), cwa_ref, cba_ref, 2 * j)
    hg = conv(_dot(xn, wg_ref[...]), cwg_ref, cbg_ref, 2 * j + 1)
    act = (jax.nn.gelu(ha) * hg).astype(BF16)
    acc_s[...] += _dot(act, wd_ref[...])

    @pl.when(j == pl.num_programs(1) - 1)
    def _():
        y = x_ref[...] + acc_s[...]
        if final_norm:
            y = _rms(y, gf_ref[...])
        o_ref[...] = y


def _ffn(x2, g, w_up, conv_w, conv_b, w_down, g_final, s, tm, tf, final_norm):
    t = x2.shape[0]
    nj = D_FF // tf
    return pl.pallas_call(
        functools.partial(_ffn_kernel, tiles_per_seq=s // tm, final_norm=final_norm),
        grid=(t // tm, nj),
        in_specs=[pl.BlockSpec((tm, D_MODEL), lambda i, j: (i, 0)),
                  pl.BlockSpec((1, D_MODEL), lambda i, j: (0, 0)),
                  pl.BlockSpec((D_MODEL, tf), lambda i, j: (0, j)),
                  pl.BlockSpec((D_MODEL, tf), lambda i, j: (0, j + nj)),
                  pl.BlockSpec((3, tf), lambda i, j: (0, j)),
                  pl.BlockSpec((3, tf), lambda i, j: (0, j + nj)),
                  pl.BlockSpec((1, tf), lambda i, j: (0, j)),
                  pl.BlockSpec((1, tf), lambda i, j: (0, j + nj)),
                  pl.BlockSpec((tf, D_MODEL), lambda i, j: (j, 0)),
                  pl.BlockSpec((1, D_MODEL), lambda i, j: (0, 0))],
        out_specs=pl.BlockSpec((tm, D_MODEL), lambda i, j: (i, 0)),
        out_shape=jax.ShapeDtypeStruct((t, D_MODEL), F32),
        scratch_shapes=[pltpu.VMEM((tm, D_MODEL), BF16), pltpu.VMEM((tm, D_MODEL), F32),
                        pltpu.VMEM((2 * nj, CARRY_ROWS, tf), F32)],
        compiler_params=_cparams("arbitrary", "arbitrary"),
        name="ffn",
    )(x2, g, w_up, w_up, conv_w, conv_w, conv_b, conv_b, w_down, g_final)


def _pad_w_in(w):
    bz_end = 2 * BRANCH_WIDTH + 3 * BRANCH_WIDTH + B_HEADS
    z1 = jnp.zeros((D_MODEL, LANES - B_HEADS), w.dtype)
    z2 = jnp.zeros((D_MODEL, IN_COLS_PAD - (w.shape[1] + LANES - B_HEADS)), w.dtype)
    return jnp.concatenate([w[:, :bz_end], z1, w[:, bz_end:], z2], axis=1).astype(BF16)


def _rope_tables(s):
    half = D_HEAD // 2
    inv_freq = ROPE_THETA ** (-jnp.arange(half, dtype=F32) / half)
    ang = jnp.arange(s, dtype=jnp.int32).astype(F32)[:, None] * inv_freq[None, :]
    cos, sin = jnp.cos(ang), jnp.sin(ang)
    cos_full = jnp.tile(cos, (1, LANES // half))
    sin_signed = jnp.tile(jnp.concatenate([-sin, sin], axis=1), (1, LANES // D_HEAD))
    return cos_full, sin_signed


def _tiles(b, s):
    return {"attn": min(256, s), "inproj": min(512, b * s), "merge": min(1024, b * s), "ffn": min(512, s)}


def kernel(x, norm_mix_g, w_in, fox_b_f, gmlp_ln_g, gmlp_ln_b, gmlp_w_s, gmlp_b_s, hgrn_lb_logits, hgrn_norm_g, diff_lambda, diff_norm_g, w_branch, w_gate, b_gate, w_out, norm_ffn_g, ffn_w_up, ffn_conv_w, ffn_conv_b, ffn_w_down, norm_final_g):
    b, s, _ = x.shape
    t = b * s
    assert s % GMLP_BLOCK == 0
    tl = _tiles(b, s)
    tb, tm_in, tm_merge, tm_ffn = tl["attn"], tl["inproj"], tl["merge"], tl["ffn"]
    x2 = x.reshape(t, D_MODEL)
    cos_full, sin_signed = _rope_tables(s)
    bf_pad = jnp.pad(fox_b_f, ((0, 0), (0, LANES - B_HEADS)))
    for l in range(DEPTH):
        lam_init = 0.8 - 0.6 * math.exp(-0.3 * l)
        p, xn = _rms_inproj(x2, norm_mix_g[l][None], _pad_w_in(w_in[l]), tm_in, 1280)
        y_a = _gmlp(p, gmlp_ln_g[l][None], gmlp_ln_b[l][None], gmlp_w_s[l], gmlp_b_s[l].T, min(256, s))
        c, ct = _foxgate(p, bf_pad[l][None], b, s, tb)
        y_b = _fox(p, c, ct, b, s, tb)
        y_c = _hgrn(p, hgrn_lb_logits, hgrn_norm_g[l][None], b, s, l)
        y_d = _diff(p, cos_full, sin_signed, diff_lambda[l], diff_norm_g[l][None], b, s, tb, lam_init)
        mixed = _merge(xn, (y_a, y_b, y_c, y_d), w_gate[l].astype(BF16), b_gate[l], w_branch[l].astype(BF16),
                       tm_merge, 512)
        x2 = _outproj(x2, mixed, w_out[l].astype(BF16), tm_merge, 1024)
        x2 = _ffn(x2, norm_ffn_g[l][None], ffn_w_up[l].astype(BF16), ffn_conv_w[l], ffn_conv_b[l][None],
                  ffn_w_down[l].astype(BF16), norm_final_g[None], s, tm_ffn, 512, l == DEPTH - 1)
    return x2.reshape(b, s, D_MODEL)
```

```python
import functools
import math

import jax
import jax.numpy as jnp
from jax import lax
from jax.experimental import pallas as pl
from jax.experimental.pallas import tpu as pltpu

F32 = jnp.float32
BF16 = jnp.bfloat16

D_MODEL = 2048
DEPTH = 2
CHUNK = 64
BRANCH_WIDTH = 512
N_BRANCH = 4
GMLP_BLOCK = 128
A_GROUPS = 4
B_HEADS = 8
C_KDIM = 128
D_HEAD = 64
D_FF = 5632
ROPE_THETA = 10000.0
EPS = 1e-6
LANES = 128
LOG2E = 1.4426950408889634

IN_COLS_PAD = 6400
COL_A = 0
COL_BQ, COL_BK, COL_BV, COL_BZ = 8, 12, 16, 20
COL_CQ, COL_CZ, COL_CI, COL_CG = 21, 25, 29, 33
COL_DQ, COL_DK, COL_DV = 37, 41, 45

VMEM_LIMIT = 56 * 1024 * 1024


def _cparams(*sem):
    return pltpu.CompilerParams(dimension_semantics=sem, vmem_limit_bytes=VMEM_LIMIT)


def _rms(x, g):
    ms = jnp.mean(x * x, axis=-1, keepdims=True)
    return x * lax.rsqrt(ms + EPS) * g


def _dot(a, b):
    return jnp.dot(a, b, preferred_element_type=F32)


def _dot_nt(a, b):
    return lax.dot_general(a, b, (((1,), (1,)), ((), ())), preferred_element_type=F32)


def _dot_tn(a, b):
    return lax.dot_general(a, b, (((0,), (0,)), ((), ())), preferred_element_type=F32)


def _tri_cumsum(tri, x):
    hi = x.astype(BF16)
    r1 = x - hi.astype(F32)
    mid = r1.astype(BF16)
    lo = (r1 - mid.astype(F32)).astype(BF16)
    return (_dot(tri, hi) + _dot(tri, mid)) + _dot(tri, lo)


def _lower_tri(n):
    return (lax.broadcasted_iota(jnp.int32, (n, n), 0) >= lax.broadcasted_iota(jnp.int32, (n, n), 1)).astype(BF16)


def _rms_inproj_kernel(x_ref, g_ref, w_ref, p_ref, xn_ref):
    @pl.when(pl.program_id(1) == 0)
    def _():
        xn_ref[...] = _rms(x_ref[...], g_ref[...]).astype(BF16)

    p_ref[...] = _dot(xn_ref[...], w_ref[...])


def _rms_inproj(x2, g, w, tm, tn):
    t, np_ = x2.shape[0], w.shape[1]
    return pl.pallas_call(
        _rms_inproj_kernel,
        grid=(t // tm, np_ // tn),
        in_specs=[pl.BlockSpec((tm, D_MODEL), lambda i, j: (i, 0)),
                  pl.BlockSpec((1, D_MODEL), lambda i, j: (0, 0)),
                  pl.BlockSpec((D_MODEL, tn), lambda i, j: (0, j))],
        out_specs=[pl.BlockSpec((tm, tn), lambda i, j: (i, j)),
                   pl.BlockSpec((tm, D_MODEL), lambda i, j: (i, 0))],
        out_shape=[jax.ShapeDtypeStruct((t, np_), F32), jax.ShapeDtypeStruct((t, D_MODEL), BF16)],
        compiler_params=_cparams("parallel", "arbitrary"),
        name="rms_inproj",
    )(x2, g, w)


def _gmlp_kernel(p_ref, lng_ref, lnb_ref, ws_ref, bs_ref, y_ref):
    ta = p_ref.shape[0]
    pa = jax.nn.gelu(p_ref[...])
    u = pa[:, :BRANCH_WIDTH]
    v = pa[:, BRANCH_WIDTH:]
    mu = jnp.mean(v, axis=-1, keepdims=True)
    d = v - mu
    var = jnp.mean(d * d, axis=-1, keepdims=True)
    vn = (d * lax.rsqrt(var + EPS) * lng_ref[...] + lnb_ref[...]).astype(BF16)
    rc = lax.broadcasted_iota(jnp.int32, (GMLP_BLOCK, GMLP_BLOCK), 0) // CHUNK
    cc = lax.broadcasted_iota(jnp.int32, (GMLP_BLOCK, GMLP_BLOCK), 1) // CHUNK
    mask = rc >= cc
    for g in range(A_GROUPS):
        wg = jnp.where(mask, ws_ref[g], 0.0).astype(BF16)
        bcol = bs_ref[:, g:g + 1]
        cs = slice(g * LANES, (g + 1) * LANES)
        for blk in range(ta // GMLP_BLOCK):
            rs = slice(blk * GMLP_BLOCK, (blk + 1) * GMLP_BLOCK)
            mixed = _dot(wg, vn[rs, cs]) + bcol
            y_ref[rs, cs] = (u[rs, cs] * mixed).astype(BF16)


def _gmlp(p, ln_g, ln_b, w_s, b_s_t, ta):
    t = p.shape[0]
    return pl.pallas_call(
        _gmlp_kernel,
        grid=(t // ta,),
        in_specs=[pl.BlockSpec((ta, 2 * BRANCH_WIDTH), lambda i: (i, 0)),
                  pl.BlockSpec((1, BRANCH_WIDTH), lambda i: (0, 0)),
                  pl.BlockSpec((1, BRANCH_WIDTH), lambda i: (0, 0)),
                  pl.BlockSpec((A_GROUPS, GMLP_BLOCK, GMLP_BLOCK), lambda i: (0, 0, 0)),
                  pl.BlockSpec((GMLP_BLOCK, A_GROUPS), lambda i: (0, 0))],
        out_specs=pl.BlockSpec((ta, BRANCH_WIDTH), lambda i: (i, 0)),
        out_shape=jax.ShapeDtypeStruct((t, BRANCH_WIDTH), BF16),
        compiler_params=_cparams("parallel"),
        name="gmlp",
    )(p, ln_g, ln_b, w_s, b_s_t)


def _foxgate_kernel(p_ref, bf_ref, c_ref, ct_ref, *, tb):
    s = p_ref.shape[0]
    seg = 256 if s % 256 == 0 else s
    z = p_ref[...] + bf_ref[...]
    ls = jnp.minimum(z, 0.0) - jnp.log1p(jnp.exp(-jnp.abs(z)))
    tri = _lower_tri(seg)
    carry = jnp.zeros((1, LANES), F32)
    for blk in range(s // seg):
        rs = slice(blk * seg, (blk + 1) * seg)
        cs = _tri_cumsum(tri, ls[rs]) + carry
        c_ref[rs, :] = cs * LOG2E
        carry = cs[seg - 1:seg, :]
    ct = c_ref[...].T
    for j in range(s // tb):
        ct_ref[0, j] = ct[:B_HEADS, j * tb:(j + 1) * tb]


def _foxgate(p, bf, b, s, tb):
    return pl.pallas_call(
        functools.partial(_foxgate_kernel, tb=tb),
        grid=(b,),
        in_specs=[pl.BlockSpec((s, LANES), lambda i: (i, COL_BZ)),
                  pl.BlockSpec((1, LANES), lambda i: (0, 0))],
        out_specs=[pl.BlockSpec((s, LANES), lambda i: (i, 0)),
                   pl.BlockSpec((1, s // tb, B_HEADS, tb), lambda i: (i, 0, 0, 0))],
        out_shape=[jax.ShapeDtypeStruct((b * s, LANES), F32),
                   jax.ShapeDtypeStruct((b, s // tb, B_HEADS, tb), F32)],
        compiler_params=_cparams("parallel"),
        name="foxgate",
    )(p, bf)


def _flash(streams, kt_ref, vb_ref, n_full, tk, diag_mask):
    tq = streams[0][0].shape[0]

    def step(j, carry, masked):
        ktj = kt_ref[j]
        vj = vb_ref[pl.ds(pl.multiple_of(j * tk, tk), tk), :]
        new = []
        for (qh, bias_fn), (m, l, acc) in zip(streams, carry):
            sc = _dot(qh, ktj)
            if bias_fn is not None:
                sc = bias_fn(sc, j)
            if masked:
                sc = jnp.where(diag_mask, sc, -jnp.inf)
            m_new = jnp.maximum(m, jnp.max(sc, axis=-1, keepdims=True))
            alpha = jnp.exp2(m - m_new)
            pr = jnp.exp2(sc - m_new)
            l = alpha * l + jnp.sum(pr, axis=-1, keepdims=True)
            acc = alpha * acc + _dot(pr.astype(BF16), vj)
            new.append((m_new, l, acc))
        return tuple(new)

    init = tuple((jnp.full((tq, 1), -jnp.inf, F32), jnp.zeros((tq, 1), F32), jnp.zeros((tq, LANES), F32))
                 for _ in streams)
    carry = lax.fori_loop(0, n_full, lambda j, c: step(j, c, False), init)
    return [acc / l for _, l, acc in step(n_full, carry, True)]


def _diag_offsets(qi, tq, tk):
    jd = (qi * tq) // tk
    return jd, jd * tk - qi * tq


def _fox_kernel(q_ref, k_ref, v_ref, c_ref, ct_ref, o_ref, kt_s, vb_s, *, tq, tk):
    s = q_ref.shape[0]
    hp = pl.program_id(1)
    for j in range(s // tk):
        kt_s[j] = k_ref[j * tk:(j + 1) * tk, :].T.astype(BF16)
    vb_s[...] = v_ref[...].astype(BF16)
    lane = lax.broadcasted_iota(jnp.int32, (1, LANES), 1)
    row_minus_col = lax.broadcasted_iota(jnp.int32, (tq, tk), 0) - lax.broadcasted_iota(jnp.int32, (tq, tk), 1)
    scale = (D_HEAD ** -0.5) * LOG2E

    def qblock(qi, _):
        rs = pl.ds(pl.multiple_of(qi * tq, tq), tq)
        q = q_ref[rs, :] * scale
        cblk = c_ref[rs, :]
        jd, off = _diag_offsets(qi, tq, tk)
        streams = []
        for hh in range(2):
            h = 2 * hp + hh
            qh = jnp.where((lane // D_HEAD) == hh, q, 0.0).astype(BF16)
            ccol = jnp.sum(jnp.where(lane == h, cblk, 0.0), axis=-1, keepdims=True)

            def bias(sc, j, h=h, ccol=ccol):
                return sc + (ccol - ct_ref[0, j, pl.ds(h, 1), :])

            streams.append((qh, bias))
        o0, o1 = _flash(streams, kt_s, vb_s, jd, tk, row_minus_col >= off)
        o_ref[rs, :] = jnp.where(lane < D_HEAD, o0, o1).astype(BF16)
        return 0

    lax.fori_loop(0, s // tq, qblock, 0)


def _fox(p, c, ct, b, s, tq, tk):
    nk = s // tk
    return pl.pallas_call(
        functools.partial(_fox_kernel, tq=tq, tk=tk),
        grid=(b, B_HEADS // 2),
        in_specs=[pl.BlockSpec((s, LANES), lambda i, h: (i, COL_BQ + h)),
                  pl.BlockSpec((s, LANES), lambda i, h: (i, COL_BK + h)),
                  pl.BlockSpec((s, LANES), lambda i, h: (i, COL_BV + h)),
                  pl.BlockSpec((s, LANES), lambda i, h: (i, 0)),
                  pl.BlockSpec((1, nk, B_HEADS, tk), lambda i, h: (i, 0, 0, 0))],
        out_specs=pl.BlockSpec((s, LANES), lambda i, h: (i, h)),
        out_shape=jax.ShapeDtypeStruct((b * s, BRANCH_WIDTH), BF16),
        scratch_shapes=[pltpu.VMEM((nk, LANES, tk), BF16), pltpu.VMEM((s, LANES), BF16)],
        compiler_params=_cparams("parallel", "arbitrary"),
        name="fox_attn",
    )(p, p, p, c, ct)


def _diff_kernel(q_ref, k_ref, v_ref, cos_ref, sin_ref, lam_ref, g_ref, o_ref, kt_s, vb_s, *, tq, tk, lam_init):
    s = q_ref.shape[0]
    lane = lax.broadcasted_iota(jnp.int32, (1, LANES), 1)
    lower_half = (lane % D_HEAD) < (D_HEAD // 2)

    def rope(x, cos, sin_signed):
        partner = jnp.where(lower_half, pltpu.roll(x, LANES - D_HEAD // 2, 1), pltpu.roll(x, D_HEAD // 2, 1))
        return x * cos + partner * sin_signed

    for j in range(s // tk):
        rs = slice(j * tk, (j + 1) * tk)
        kt_s[j] = rope(k_ref[rs, :], cos_ref[rs, :], sin_ref[rs, :]).T.astype(BF16)
    vb_s[...] = v_ref[...].astype(BF16)

    lp = lam_ref[...]
    lam = (jnp.exp(jnp.sum(lp[0:1] * lp[1:2], axis=-1, keepdims=True))
           - jnp.exp(jnp.sum(lp[2:3] * lp[3:4], axis=-1, keepdims=True)) + lam_init)
    chunk_row_minus_col = (lax.broadcasted_iota(jnp.int32, (tq, tk), 0) // CHUNK
                           - lax.broadcasted_iota(jnp.int32, (tq, tk), 1) // CHUNK)
    scale = (D_HEAD ** -0.5) * LOG2E

    def qblock(qi, _):
        rs = pl.ds(pl.multiple_of(qi * tq, tq), tq)
        q = rope(q_ref[rs, :], cos_ref[rs, :], sin_ref[rs, :]) * scale
        q1 = jnp.where(lane < D_HEAD, q, 0.0).astype(BF16)
        q2 = jnp.where(lane >= D_HEAD, q, 0.0).astype(BF16)
        jd, off = _diag_offsets(qi, tq, tk)
        o1, o2 = _flash([(q1, None), (q2, None)], kt_s, vb_s, jd, tk, chunk_row_minus_col >= off // CHUNK)
        o = o1 - lam * o2
        o_ref[rs, :] = (_rms(o, g_ref[...]) * (1.0 - lam_init)).astype(BF16)
        return 0

    lax.fori_loop(0, s // tq, qblock, 0)


def _diff(p, cos, sin_signed, lam_p, norm_g, b, s, tq, tk, lam_init):
    nb = s // tk
    return pl.pallas_call(
        functools.partial(_diff_kernel, tq=tq, tk=tk, lam_init=lam_init),
        grid=(b, BRANCH_WIDTH // LANES),
        in_specs=[pl.BlockSpec((s, LANES), lambda i, h: (i, COL_DQ + h)),
                  pl.BlockSpec((s, LANES), lambda i, h: (i, COL_DK + h)),
                  pl.BlockSpec((s, LANES), lambda i, h: (i, COL_DV + h)),
                  pl.BlockSpec((s, LANES), lambda i, h: (0, 0)),
                  pl.BlockSpec((s, LANES), lambda i, h: (0, 0)),
                  pl.BlockSpec((4, D_HEAD), lambda i, h: (0, 0)),
                  pl.BlockSpec((1, LANES), lambda i, h: (0, 0))],
        out_specs=pl.BlockSpec((s, LANES), lambda i, h: (i, h)),
        out_shape=jax.ShapeDtypeStruct((b * s, BRANCH_WIDTH), BF16),
        scratch_shapes=[pltpu.VMEM((nb, LANES, tk), BF16), pltpu.VMEM((s, LANES), BF16)],
        compiler_params=_cparams("parallel", "arbitrary"),
        name="diff_attn",
    )(p, p, p, cos, sin_signed, lam_p, norm_g)


SUB = 16


def _hgrn_kernel(q_ref, z_ref, i_ref, g_ref, lbl_ref, ng_ref, o_ref, *, layer):
    s = q_ref.shape[0]
    lbl = lbl_ref[...]
    e = jnp.exp(lbl - jnp.max(lbl, axis=0, keepdims=True))
    pr = e / jnp.sum(e, axis=0, keepdims=True)
    lb = pr[0:1] - pr[0:1]
    for r in range(1, layer + 1):
        lb = lb + pr[r:r + 1]
    oml = 1.0 - lb
    tri = _lower_tri(CHUNK)
    lane_s = lax.broadcasted_iota(jnp.int32, (SUB, CHUNK), 1)
    row_s = lax.broadcasted_iota(jnp.int32, (SUB, CHUNK), 0)
    qscale = C_KDIM ** -0.5

    def chunk_body(n, st):
        rs = pl.ds(pl.multiple_of(n * CHUNK, CHUNK), CHUNK)
        z = z_ref[rs, :]
        f = lb + oml * jax.nn.sigmoid(z)
        kk = oml * jax.nn.sigmoid(-z)
        cum = _tri_cumsum(tri, jnp.log(f))
        q = q_ref[rs, :] * qscale
        vb = i_ref[rs, :].astype(BF16)
        inter = _dot_nt((q * jnp.exp(cum)).astype(BF16), st.astype(BF16))
        blocks = []
        for bi in range(CHUNK // SUB):
            bs = slice(bi * SUB, (bi + 1) * SUB)
            qb, cb, kb = q[bs], cum[bs], kk[bs]
            if bi == 0:
                sc = jnp.zeros((SUB, CHUNK), F32)
            else:
                ref_row = cum[bi * SUB - 1:bi * SUB]
                qi_ = (qb * jnp.exp(cb - ref_row)).astype(BF16)
                ki_ = (kk * jnp.exp(jnp.minimum(ref_row - cum, 0.0))).astype(BF16)
                sc = jnp.where(lane_s < bi * SUB, _dot_nt(qi_, ki_), 0.0)
            for si in range(SUB):
                x = qb * jnp.exp(jnp.minimum(cb - cb[si:si + 1], 0.0)) * kb[si:si + 1]
                col = jnp.sum(x, axis=-1, keepdims=True)
                sc = jnp.where((lane_s == bi * SUB + si) & (row_s >= si), col, sc)
            blocks.append(sc)
        scores = jnp.concatenate(blocks, axis=0)
        out = inter + _dot(scores.astype(BF16), vb)
        last = cum[CHUNK - 1:CHUNK]
        kdec = (kk * jnp.exp(last - cum)).astype(BF16)
        st_new = st * jnp.exp(last) + _dot_tn(vb, kdec)
        y = _rms(out, ng_ref[...]) * jax.nn.sigmoid(g_ref[rs, :])
        o_ref[rs, :] = y.astype(BF16)
        return st_new

    lax.fori_loop(0, s // CHUNK, chunk_body, jnp.zeros((LANES, C_KDIM), F32))


def _hgrn(p, lb_logits, norm_g, b, s, layer):
    return pl.pallas_call(
        functools.partial(_hgrn_kernel, layer=layer),
        grid=(b, BRANCH_WIDTH // LANES),
        in_specs=[pl.BlockSpec((s, LANES), lambda i, h: (i, COL_CQ + h)),
                  pl.BlockSpec((s, LANES), lambda i, h: (i, COL_CZ + h)),
                  pl.BlockSpec((s, LANES), lambda i, h: (i, COL_CI + h)),
                  pl.BlockSpec((s, LANES), lambda i, h: (i, COL_CG + h)),
                  pl.BlockSpec((DEPTH, LANES), lambda i, h: (0, h)),
                  pl.BlockSpec((1, LANES), lambda i, h: (0, 0))],
        out_specs=pl.BlockSpec((s, LANES), lambda i, h: (i, h)),
        out_shape=jax.ShapeDtypeStruct((b * s, BRANCH_WIDTH), BF16),
        compiler_params=_cparams("parallel", "arbitrary"),
        name="hgrn",
    )(p, p, p, p, lb_logits, norm_g)


def _merge_kernel(xn_ref, ya_ref, yb_ref, yc_ref, yd_ref, wg_ref, bg_ref, wb_ref, o_ref):
    xn = xn_ref[...]
    acc = None
    for n, y_ref in enumerate((ya_ref, yb_ref, yc_ref, yd_ref)):
        gate = jax.nn.sigmoid(_dot(xn, wg_ref[n]) + bg_ref[n:n + 1, :])
        term = gate * _dot(y_ref[...], wb_ref[n])
        acc = term if acc is None else acc + term
    o_ref[...] = acc.astype(BF16)


def _merge(xn, ys, wg, bg, wb, tm, tn):
    t = xn.shape[0]
    yspec = pl.BlockSpec((tm, BRANCH_WIDTH), lambda i, j: (i, 0))
    return pl.pallas_call(
        _merge_kernel,
        grid=(t // tm, D_MODEL // tn),
        in_specs=[pl.BlockSpec((tm, D_MODEL), lambda i, j: (i, 0)), yspec, yspec, yspec, yspec,
                  pl.BlockSpec((N_BRANCH, D_MODEL, tn), lambda i, j: (0, 0, j)),
                  pl.BlockSpec((N_BRANCH, tn), lambda i, j: (0, j)),
                  pl.BlockSpec((N_BRANCH, BRANCH_WIDTH, tn), lambda i, j: (0, 0, j))],
        out_specs=pl.BlockSpec((tm, tn), lambda i, j: (i, j)),
        out_shape=jax.ShapeDtypeStruct((t, D_MODEL), BF16),
        compiler_params=_cparams("parallel", "arbitrary"),
        name="merge",
    )(xn, *ys, wg, bg, wb)


def _outproj_kernel(x_ref, m_ref, w_ref, o_ref):
    o_ref[...] = x_ref[...] + _dot(m_ref[...], w_ref[...])


def _outproj(x2, mixed, w, tm, tn):
    t = x2.shape[0]
    return pl.pallas_call(
        _outproj_kernel,
        grid=(t // tm, D_MODEL // tn),
        in_specs=[pl.BlockSpec((tm, tn), lambda i, j: (i, j)),
                  pl.BlockSpec((tm, D_MODEL), lambda i, j: (i, 0)),
                  pl.BlockSpec((D_MODEL, tn), lambda i, j: (0, j))],
        out_specs=pl.BlockSpec((tm, tn), lambda i, j: (i, j)),
        out_shape=jax.ShapeDtypeStruct((t, D_MODEL), F32),
        compiler_params=_cparams("parallel", "arbitrary"),
        name="outproj",
    )(x2, mixed, w)


CARRY_ROWS = 8


def _ffn_kernel(x_ref, g_ref, wa_ref, wg_ref, cwa_ref, cwg_ref, cba_ref, cbg_ref, wd_ref, gf_ref, o_ref,
                xn_s, carry_s, *, tiles_per_seq, final_norm):
    i = pl.program_id(0)
    j = pl.program_id(1)
    tm, tf = xn_s.shape[0], wa_ref.shape[1]

    @pl.when((i == 0) & (j == 0))
    def _():
        carry_s[...] = jnp.zeros_like(carry_s)

    @pl.when(j == 0)
    def _():
        x = x_ref[...]
        xn_s[...] = _rms(x, g_ref[...]).astype(BF16)
        o_ref[...] = x

    xn = xn_s[...]
    seq_start = (i % tiles_per_seq) == 0
    r8 = lax.broadcasted_iota(jnp.int32, (CARRY_ROWS, tf), 0)

    def conv(h, cw_ref, cb_ref, slot):
        prev = jnp.where(seq_start, 0.0, carry_s[slot])
        carry_s[slot] = h[tm - CARRY_ROWS:tm]
        h1 = pltpu.roll(h, 1, 0)
        h2 = pltpu.roll(h, 2, 0)
        p1 = pltpu.roll(prev, 1, 0)
        p2 = pltpu.roll(prev, 2, 0)
        h1 = jnp.concatenate([jnp.where(r8 < 1, p1, h1[:CARRY_ROWS]), h1[CARRY_ROWS:]], axis=0)
        h2 = jnp.concatenate([jnp.where(r8 < 2, p2, h2[:CARRY_ROWS]), h2[CARRY_ROWS:]], axis=0)
        cw = cw_ref[...]
        return cw[0:1] * h2 + cw[1:2] * h1 + cw[2:3] * h + cb_ref[...]

    ha = conv(_dot(xn, wa_ref[...]), cwa_ref, cba_ref, 2 * j)
    hg = conv(_dot(xn, wg_ref[...]), cwg_ref, cbg_ref, 2 * j + 1)
    act = (jax.nn.gelu(ha) * hg).astype(BF16)
    o_ref[...] += _dot(act, wd_ref[...])

    if final_norm:
        @pl.when(j == pl.num_programs(1) - 1)
        def _():
            o_ref[...] = _rms(o_ref[...], gf_ref[...])


def _ffn(x2, g, w_up, conv_w, conv_b, w_down, g_final, s, tm, tf, final_norm):
    t = x2.shape[0]
    nj = D_FF // tf
    return pl.pallas_call(
        functools.partial(_ffn_kernel, tiles_per_seq=s // tm, final_norm=final_norm),
        grid=(t // tm, nj),
        in_specs=[pl.BlockSpec((tm, D_MODEL), lambda i, j: (i, 0), pipeline_mode=pl.Buffered(1)),
                  pl.BlockSpec((1, D_MODEL), lambda i, j: (0, 0)),
                  pl.BlockSpec((D_MODEL, tf), lambda i, j: (0, j)),
                  pl.BlockSpec((D_MODEL, tf), lambda i, j: (0, j + nj)),
                  pl.BlockSpec((3, tf), lambda i, j: (0, j)),
                  pl.BlockSpec((3, tf), lambda i, j: (0, j + nj)),
                  pl.BlockSpec((1, tf), lambda i, j: (0, j)),
                  pl.BlockSpec((1, tf), lambda i, j: (0, j + nj)),
                  pl.BlockSpec((tf, D_MODEL), lambda i, j: (j, 0)),
                  pl.BlockSpec((1, D_MODEL), lambda i, j: (0, 0))],
        out_specs=pl.BlockSpec((tm, D_MODEL), lambda i, j: (i, 0)),
        out_shape=jax.ShapeDtypeStruct((t, D_MODEL), F32),
        scratch_shapes=[pltpu.VMEM((tm, D_MODEL), BF16), pltpu.VMEM((2 * nj, CARRY_ROWS, tf), F32)],
        compiler_params=_cparams("arbitrary", "arbitrary"),
        name="ffn",
    )(x2, g, w_up, w_up, conv_w, conv_w, conv_b, conv_b, w_down, g_final)


def _pad_w_in(w):
    bz_end = 2 * BRANCH_WIDTH + 3 * BRANCH_WIDTH + B_HEADS
    z1 = jnp.zeros((D_MODEL, LANES - B_HEADS), w.dtype)
    z2 = jnp.zeros((D_MODEL, IN_COLS_PAD - (w.shape[1] + LANES - B_HEADS)), w.dtype)
    return jnp.concatenate([w[:, :bz_end], z1, w[:, bz_end:], z2], axis=1).astype(BF16)


def _rope_tables(s):
    half = D_HEAD // 2
    inv_freq = ROPE_THETA ** (-jnp.arange(half, dtype=F32) / half)
    ang = jnp.arange(s, dtype=jnp.int32).astype(F32)[:, None] * inv_freq[None, :]
    cos, sin = jnp.cos(ang), jnp.sin(ang)
    cos_full = jnp.tile(cos, (1, LANES // half))
    sin_signed = jnp.tile(jnp.concatenate([-sin, sin], axis=1), (1, LANES // D_HEAD))
    return cos_full, sin_signed


def _tiles(b, s):
    return {"attn_q": min(512, s), "attn_k": min(512, s), "inproj": min(1024, b * s), "merge": min(1024, b * s),
            "ffn": min(1024, s)}


def kernel(x, norm_mix_g, w_in, fox_b_f, gmlp_ln_g, gmlp_ln_b, gmlp_w_s, gmlp_b_s, hgrn_lb_logits, hgrn_norm_g, diff_lambda, diff_norm_g, w_branch, w_gate, b_gate, w_out, norm_ffn_g, ffn_w_up, ffn_conv_w, ffn_conv_b, ffn_w_down, norm_final_g):
    b, s, _ = x.shape
    t = b * s
    assert s % GMLP_BLOCK == 0
    tl = _tiles(b, s)
    tq, tk, tm_in, tm_merge, tm_ffn = tl["attn_q"], tl["attn_k"], tl["inproj"], tl["merge"], tl["ffn"]
    assert tk % tq == 0 and tq % CHUNK == 0
    x2 = x.reshape(t, D_MODEL)
    cos_full, sin_signed = _rope_tables(s)
    bf_pad = jnp.pad(fox_b_f, ((0, 0), (0, LANES - B_HEADS)))
    for l in range(DEPTH):
        lam_init = 0.8 - 0.6 * math.exp(-0.3 * l)
        p, xn = _rms_inproj(x2, norm_mix_g[l][None], _pad_w_in(w_in[l]), tm_in, 1280)
        y_a = _gmlp(p, gmlp_ln_g[l][None], gmlp_ln_b[l][None], gmlp_w_s[l], gmlp_b_s[l].T, min(256, s))
        c, ct = _foxgate(p, bf_pad[l][None], b, s, tk)
        y_b = _fox(p, c, ct, b, s, tq, tk)
        y_c = _hgrn(p, hgrn_lb_logits, hgrn_norm_g[l][None], b, s, l)
        y_d = _diff(p, cos_full, sin_signed, diff_lambda[l], diff_norm_g[l][None], b, s, tq, tk, lam_init)
        mixed = _merge(xn, (y_a, y_b, y_c, y_d), w_gate[l].astype(BF16), b_gate[l], w_branch[l].astype(BF16),
                       tm_merge, 512)
        x2 = _outproj(x2, mixed, w_out[l].astype(BF16), tm_merge, 1024)
        x2 = _ffn(x2, norm_ffn_g[l][None], ffn_w_up[l].astype(BF16), ffn_conv_w[l], ffn_conv_b[l][None],
                  ffn_w_down[l].astype(BF16), norm_final_g[None], s, tm_ffn, 512, l == DEPTH - 1)
    return x2.reshape(b, s, D_MODEL)
```

```python
import functools
import math

import jax
import jax.numpy as jnp
from jax import lax
from jax.experimental import pallas as pl
from jax.experimental.pallas import tpu as pltpu

F32 = jnp.float32
BF16 = jnp.bfloat16

D_MODEL = 2048
DEPTH = 2
CHUNK = 64
BRANCH_WIDTH = 512
N_BRANCH = 4
GMLP_BLOCK = 128
A_GROUPS = 4
B_HEADS = 8
C_KDIM = 128
D_HEAD = 64
D_FF = 5632
ROPE_THETA = 10000.0
EPS = 1e-6
LANES = 128
LOG2E = 1.4426950408889634

IN_COLS_PAD = 6400
IN_TILE = 1280
COL_A = 0
COL_BQ, COL_BK, COL_BV = 8, 12, 16
COL_CQ, COL_CI, COL_CG = 20, 24, 28
COL_DQ, COL_DK = 32, 36
COL_CZ, COL_BZ, COL_DV = 40, 44, 45
COL32_CZ, COL32_BZ = COL_CZ - 40, COL_BZ - 40

VMEM_LIMIT = 56 * 1024 * 1024


def _cparams(*sem):
    return pltpu.CompilerParams(dimension_semantics=sem, vmem_limit_bytes=VMEM_LIMIT)


def _rms(x, g):
    ms = jnp.mean(x * x, axis=-1, keepdims=True)
    return x * lax.rsqrt(ms + EPS) * g


def _dot(a, b):
    return jnp.dot(a, b, preferred_element_type=F32)


def _dot_nt(a, b):
    return lax.dot_general(a, b, (((1,), (1,)), ((), ())), preferred_element_type=F32)


def _dot_tn(a, b):
    return lax.dot_general(a, b, (((0,), (0,)), ((), ())), preferred_element_type=F32)


def _tri_cumsum(tri, x):
    hi = x.astype(BF16)
    r1 = x - hi.astype(F32)
    mid = r1.astype(BF16)
    lo = (r1 - mid.astype(F32)).astype(BF16)
    return (_dot(tri, hi) + _dot(tri, mid)) + _dot(tri, lo)


def _lower_tri(n):
    return (lax.broadcasted_iota(jnp.int32, (n, n), 0) >= lax.broadcasted_iota(jnp.int32, (n, n), 1)).astype(BF16)


def _rms_inproj_kernel(x_ref, g_ref, w_ref, p_ref, p32_ref, xn_ref):
    @pl.when(pl.program_id(1) == 0)
    def _():
        xn_ref[...] = _rms(x_ref[...], g_ref[...]).astype(BF16)

    acc = _dot(xn_ref[...], w_ref[...])
    p_ref[...] = acc.astype(BF16)

    @pl.when(pl.program_id(1) == pl.num_programs(1) - 1)
    def _():
        p32_ref[...] = acc


def _rms_inproj(x2, g, w, tm, tn):
    t, np_ = x2.shape[0], w.shape[1]
    return pl.pallas_call(
        _rms_inproj_kernel,
        grid=(t // tm, np_ // tn),
        in_specs=[pl.BlockSpec((tm, D_MODEL), lambda i, j: (i, 0)),
                  pl.BlockSpec((1, D_MODEL), lambda i, j: (0, 0)),
                  pl.BlockSpec((D_MODEL, tn), lambda i, j: (0, j))],
        out_specs=[pl.BlockSpec((tm, tn), lambda i, j: (i, j)),
                   pl.BlockSpec((tm, tn), lambda i, j: (i, 0)),
                   pl.BlockSpec((tm, D_MODEL), lambda i, j: (i, 0))],
        out_shape=[jax.ShapeDtypeStruct((t, np_), BF16), jax.ShapeDtypeStruct((t, tn), F32),
                   jax.ShapeDtypeStruct((t, D_MODEL), BF16)],
        compiler_params=_cparams("parallel", "arbitrary"),
        name="rms_inproj",
    )(x2, g, w)


def _gmlp_kernel(p_ref, lng_ref, lnb_ref, ws_ref, bs_ref, y_ref):
    ta = p_ref.shape[0]
    pa = jax.nn.gelu(p_ref[...].astype(F32))
    u = pa[:, :BRANCH_WIDTH]
    v = pa[:, BRANCH_WIDTH:]
    mu = jnp.mean(v, axis=-1, keepdims=True)
    d = v - mu
    var = jnp.mean(d * d, axis=-1, keepdims=True)
    vn = (d * lax.rsqrt(var + EPS) * lng_ref[...] + lnb_ref[...]).astype(BF16)
    rc = lax.broadcasted_iota(jnp.int32, (GMLP_BLOCK, GMLP_BLOCK), 0) // CHUNK
    cc = lax.broadcasted_iota(jnp.int32, (GMLP_BLOCK, GMLP_BLOCK), 1) // CHUNK
    mask = rc >= cc
    for g in range(A_GROUPS):
        wg = jnp.where(mask, ws_ref[g], 0.0).astype(BF16)
        bcol = bs_ref[:, g:g + 1]
        cs = slice(g * LANES, (g + 1) * LANES)
        for blk in range(ta // GMLP_BLOCK):
            rs = slice(blk * GMLP_BLOCK, (blk + 1) * GMLP_BLOCK)
            mixed = _dot(wg, vn[rs, cs]) + bcol
            y_ref[rs, cs] = (u[rs, cs] * mixed).astype(BF16)


def _gmlp(p, ln_g, ln_b, w_s, b_s_t, ta):
    t = p.shape[0]
    return pl.pallas_call(
        _gmlp_kernel,
        grid=(t // ta,),
        in_specs=[pl.BlockSpec((ta, 2 * BRANCH_WIDTH), lambda i: (i, 0)),
                  pl.BlockSpec((1, BRANCH_WIDTH), lambda i: (0, 0)),
                  pl.BlockSpec((1, BRANCH_WIDTH), lambda i: (0, 0)),
                  pl.BlockSpec((A_GROUPS, GMLP_BLOCK, GMLP_BLOCK), lambda i: (0, 0, 0)),
                  pl.BlockSpec((GMLP_BLOCK, A_GROUPS), lambda i: (0, 0))],
        out_specs=pl.BlockSpec((ta, BRANCH_WIDTH), lambda i: (i, 0)),
        out_shape=jax.ShapeDtypeStruct((t, BRANCH_WIDTH), BF16),
        compiler_params=_cparams("parallel"),
        name="gmlp",
    )(p, ln_g, ln_b, w_s, b_s_t)


def _foxgate_kernel(p_ref, bf_ref, c_ref, ct_ref, *, tb):
    s = p_ref.shape[0]
    seg = 256 if s % 256 == 0 else s
    z = p_ref[...] + bf_ref[...]
    ls = jnp.minimum(z, 0.0) - jnp.log1p(jnp.exp(-jnp.abs(z)))
    tri = _lower_tri(seg)
    carry = jnp.zeros((1, LANES), F32)
    for blk in range(s // seg):
        rs = slice(blk * seg, (blk + 1) * seg)
        cs = _tri_cumsum(tri, ls[rs]) + carry
        c_ref[rs, :] = cs * LOG2E
        carry = cs[seg - 1:seg, :]
    ct = c_ref[...].T
    for j in range(s // tb):
        ct_ref[0, j] = ct[:B_HEADS, j * tb:(j + 1) * tb]


def _foxgate(p, bf, b, s, tb):
    return pl.pallas_call(
        functools.partial(_foxgate_kernel, tb=tb),
        grid=(b,),
        in_specs=[pl.BlockSpec((s, LANES), lambda i: (i, COL32_BZ)),
                  pl.BlockSpec((1, LANES), lambda i: (0, 0))],
        out_specs=[pl.BlockSpec((s, LANES), lambda i: (i, 0)),
                   pl.BlockSpec((1, s // tb, B_HEADS, tb), lambda i: (i, 0, 0, 0))],
        out_shape=[jax.ShapeDtypeStruct((b * s, LANES), F32),
                   jax.ShapeDtypeStruct((b, s // tb, B_HEADS, tb), F32)],
        compiler_params=_cparams("parallel"),
        name="foxgate",
    )(p, bf)


def _flash(streams, kt_ref, vb_ref, n_full, tk, diag_mask):
    tq = streams[0][0].shape[0]

    def step(j, carry, masked):
        ktj = kt_ref[j]
        vj = vb_ref[pl.ds(pl.multiple_of(j * tk, tk), tk), :]
        new = []
        for (qh, bias_fn), (m, l, acc) in zip(streams, carry):
            sc = _dot(qh, ktj)
            if bias_fn is not None:
                sc = bias_fn(sc, j)
            if masked:
                sc = jnp.where(diag_mask, sc, -jnp.inf)
            m_new = jnp.maximum(m, jnp.max(sc, axis=-1, keepdims=True))
            alpha = jnp.exp2(m - m_new)
            pr = jnp.exp2(sc - m_new)
            l = alpha * l + jnp.sum(pr, axis=-1, keepdims=True)
            acc = alpha * acc + _dot(pr.astype(BF16), vj)
            new.append((m_new, l, acc))
        return tuple(new)

    init = tuple((jnp.full((tq, 1), -jnp.inf, F32), jnp.zeros((tq, 1), F32), jnp.zeros((tq, LANES), F32))
                 for _ in streams)
    carry = lax.fori_loop(0, n_full, lambda j, c: step(j, c, False), init)
    return [acc / l for _, l, acc in step(n_full, carry, True)]


def _diag_offsets(qi, tq, tk):
    jd = (qi * tq) // tk
    return jd, jd * tk - qi * tq


def _fox_kernel(q_ref, k_ref, v_ref, c_ref, ct_ref, o_ref, kt_s, *, tq, tk):
    s = q_ref.shape[0]
    hp = pl.program_id(1)
    for j in range(s // tk):
        kt_s[j] = k_ref[j * tk:(j + 1) * tk, :].astype(F32).T.astype(BF16)
    lane = lax.broadcasted_iota(jnp.int32, (1, LANES), 1)
    row_minus_col = lax.broadcasted_iota(jnp.int32, (tq, tk), 0) - lax.broadcasted_iota(jnp.int32, (tq, tk), 1)
    scale = (D_HEAD ** -0.5) * LOG2E

    def qblock(qi, _):
        rs = pl.ds(pl.multiple_of(qi * tq, tq), tq)
        q = q_ref[rs, :].astype(F32) * scale
        cblk = c_ref[rs, :]
        jd, off = _diag_offsets(qi, tq, tk)
        streams = []
        for hh in range(2):
            h = 2 * hp + hh
            qh = jnp.where((lane // D_HEAD) == hh, q, 0.0).astype(BF16)
            ccol = jnp.sum(jnp.where(lane == h, cblk, 0.0), axis=-1, keepdims=True)

            def bias(sc, j, h=h, ccol=ccol):
                return sc + (ccol - ct_ref[0, j, pl.ds(h, 1), :])

            streams.append((qh, bias))
        o0, o1 = _flash(streams, kt_s, v_ref, jd, tk, row_minus_col >= off)
        o_ref[rs, :] = jnp.where(lane < D_HEAD, o0, o1).astype(BF16)
        return 0

    lax.fori_loop(0, s // tq, qblock, 0)


def _fox(p, c, ct, b, s, tq, tk):
    nk = s // tk
    return pl.pallas_call(
        functools.partial(_fox_kernel, tq=tq, tk=tk),
        grid=(b, B_HEADS // 2),
        in_specs=[pl.BlockSpec((s, LANES), lambda i, h: (i, COL_BQ + h)),
                  pl.BlockSpec((s, LANES), lambda i, h: (i, COL_BK + h)),
                  pl.BlockSpec((s, LANES), lambda i, h: (i, COL_BV + h)),
                  pl.BlockSpec((s, LANES), lambda i, h: (i, 0)),
                  pl.BlockSpec((1, nk, B_HEADS, tk), lambda i, h: (i, 0, 0, 0))],
        out_specs=pl.BlockSpec((s, LANES), lambda i, h: (i, h)),
        out_shape=jax.ShapeDtypeStruct((b * s, BRANCH_WIDTH), BF16),
        scratch_shapes=[pltpu.VMEM((nk, LANES, tk), BF16)],
        compiler_params=_cparams("parallel", "arbitrary"),
        name="fox_attn",
    )(p, p, p, c, ct)


def _diff_kernel(q_ref, k_ref, v_ref, cos_ref, sin_ref, lam_ref, g_ref, o_ref, kt_s, *, tq, tk, lam_init):
    s = q_ref.shape[0]
    lane = lax.broadcasted_iota(jnp.int32, (1, LANES), 1)
    lower_half = (lane % D_HEAD) < (D_HEAD // 2)

    def rope(x, cos, sin_signed):
        partner = jnp.where(lower_half, pltpu.roll(x, LANES - D_HEAD // 2, 1), pltpu.roll(x, D_HEAD // 2, 1))
        return x * cos + partner * sin_signed

    for j in range(s // tk):
        rs = slice(j * tk, (j + 1) * tk)
        kt_s[j] = rope(k_ref[rs, :].astype(F32), cos_ref[rs, :], sin_ref[rs, :]).T.astype(BF16)

    lp = lam_ref[...]
    lam = (jnp.exp(jnp.sum(lp[0:1] * lp[1:2], axis=-1, keepdims=True))
           - jnp.exp(jnp.sum(lp[2:3] * lp[3:4], axis=-1, keepdims=True)) + lam_init)
    chunk_row_minus_col = (lax.broadcasted_iota(jnp.int32, (tq, tk), 0) // CHUNK
                           - lax.broadcasted_iota(jnp.int32, (tq, tk), 1) // CHUNK)
    scale = (D_HEAD ** -0.5) * LOG2E

    def qblock(qi, _):
        rs = pl.ds(pl.multiple_of(qi * tq, tq), tq)
        q = rope(q_ref[rs, :].astype(F32), cos_ref[rs, :], sin_ref[rs, :]) * scale
        q1 = jnp.where(lane < D_HEAD, q, 0.0).astype(BF16)
        q2 = jnp.where(lane >= D_HEAD, q, 0.0).astype(BF16)
        jd, off = _diag_offsets(qi, tq, tk)
        o1, o2 = _flash([(q1, None), (q2, None)], kt_s, v_ref, jd, tk, chunk_row_minus_col >= off // CHUNK)
        o = o1 - lam * o2
        o_ref[rs, :] = (_rms(o, g_ref[...]) * (1.0 - lam_init)).astype(BF16)
        return 0

    lax.fori_loop(0, s // tq, qblock, 0)


def _diff(p, cos, sin_signed, lam_p, norm_g, b, s, tq, tk, lam_init):
    nb = s // tk
    return pl.pallas_call(
        functools.partial(_diff_kernel, tq=tq, tk=tk, lam_init=lam_init),
        grid=(b, BRANCH_WIDTH // LANES),
        in_specs=[pl.BlockSpec((s, LANES), lambda i, h: (i, COL_DQ + h)),
                  pl.BlockSpec((s, LANES), lambda i, h: (i, COL_DK + h)),
                  pl.BlockSpec((s, LANES), lambda i, h: (i, COL_DV + h)),
                  pl.BlockSpec((s, LANES), lambda i, h: (0, 0)),
                  pl.BlockSpec((s, LANES), lambda i, h: (0, 0)),
                  pl.BlockSpec((4, D_HEAD), lambda i, h: (0, 0)),
                  pl.BlockSpec((1, LANES), lambda i, h: (0, 0))],
        out_specs=pl.BlockSpec((s, LANES), lambda i, h: (i, h)),
        out_shape=jax.ShapeDtypeStruct((b * s, BRANCH_WIDTH), BF16),
        scratch_shapes=[pltpu.VMEM((nb, LANES, tk), BF16)],
        compiler_params=_cparams("parallel", "arbitrary"),
        name="diff_attn",
    )(p, p, p, cos, sin_signed, lam_p, norm_g)


SUB = 16
HALF = 8


def _hgrn_kernel(q_ref, z_ref, i_ref, g_ref, lbl_ref, ng_ref, o_ref, *, layer):
    s = q_ref.shape[0]
    heads = q_ref.shape[1] // LANES
    lbl = lbl_ref[...]
    e = jnp.exp(lbl - jnp.max(lbl, axis=0, keepdims=True))
    pr = e / jnp.sum(e, axis=0, keepdims=True)
    lb_all = pr[0:1] - pr[0:1]
    for r in range(1, layer + 1):
        lb_all = lb_all + pr[r:r + 1]
    tri = _lower_tri(CHUNK)
    lane_s = lax.broadcasted_iota(jnp.int32, (SUB, CHUNK), 1)
    lane_h = lax.broadcasted_iota(jnp.int32, (HALF, CHUNK), 1)
    row_h = lax.broadcasted_iota(jnp.int32, (HALF, CHUNK), 0)
    qscale = C_KDIM ** -0.5

    def head_chunk(rs, cs, st):
        lb = lb_all[:, cs]
        oml = 1.0 - lb
        z = z_ref[rs, cs]
        f = lb + oml * jax.nn.sigmoid(z)
        kk = oml * jax.nn.sigmoid(-z)
        cum = _tri_cumsum(tri, jnp.log(f)) * LOG2E
        q = q_ref[rs, cs].astype(F32) * qscale
        vb = i_ref[rs, cs].astype(BF16)
        inter = _dot_nt((q * jnp.exp2(cum)).astype(BF16), st.astype(BF16))
        blocks = []
        for bi in range(CHUNK // SUB):
            b0 = bi * SUB
            if bi == 0:
                halves = [jnp.zeros((HALF, CHUNK), F32)] * 2
            else:
                ref_row = cum[b0 - 1:b0]
                qi_ = (q[b0:b0 + SUB] * jnp.exp2(cum[b0:b0 + SUB] - ref_row)).astype(BF16)
                ki_ = (kk * jnp.exp2(jnp.minimum(ref_row - cum, 0.0))).astype(BF16)
                sc = jnp.where(lane_s < b0, _dot_nt(qi_, ki_), 0.0)
                halves = [sc[:HALF], sc[HALF:]]
            for si in range(SUB):
                c_s, k_s = cum[b0 + si:b0 + si + 1], kk[b0 + si:b0 + si + 1]
                for hf in range(si // HALF, 2):
                    r0 = b0 + hf * HALF
                    x = q[r0:r0 + HALF] * jnp.exp2(cum[r0:r0 + HALF] - c_s) * k_s
                    col = jnp.sum(x, axis=-1, keepdims=True)
                    halves[hf] = jnp.where(lane_h == b0 + si, col, halves[hf])
            for hf in range(2):
                blocks.append(jnp.where(lane_h <= row_h + (b0 + hf * HALF), halves[hf], 0.0))
        scores = jnp.concatenate(blocks, axis=0)
        out = inter + _dot(scores.astype(BF16), vb)
        last = cum[CHUNK - 1:CHUNK]
        kdec = (kk * jnp.exp2(last - cum)).astype(BF16)
        st_new = st * jnp.exp2(last) + _dot_tn(vb, kdec)
        y = _rms(out, ng_ref[...]) * jax.nn.sigmoid(g_ref[rs, cs].astype(F32))
        o_ref[rs, cs] = y.astype(BF16)
        return st_new

    def chunk_body(n, states):
        rs = pl.ds(pl.multiple_of(n * CHUNK, CHUNK), CHUNK)
        return tuple(head_chunk(rs, slice(hd * LANES, (hd + 1) * LANES), st) for hd, st in enumerate(states))

    lax.fori_loop(0, s // CHUNK, chunk_body, tuple(jnp.zeros((LANES, C_KDIM), F32) for _ in range(heads)))


def _hgrn(p, p32, lb_logits, norm_g, b, s, layer, heads):
    hw = heads * LANES
    return pl.pallas_call(
        functools.partial(_hgrn_kernel, layer=layer),
        grid=(b, BRANCH_WIDTH // hw),
        in_specs=[pl.BlockSpec((s, hw), lambda i, h: (i, COL_CQ // heads + h)),
                  pl.BlockSpec((s, hw), lambda i, h: (i, COL32_CZ // heads + h)),
                  pl.BlockSpec((s, hw), lambda i, h: (i, COL_CI // heads + h)),
                  pl.BlockSpec((s, hw), lambda i, h: (i, COL_CG // heads + h)),
                  pl.BlockSpec((DEPTH, hw), lambda i, h: (0, h)),
                  pl.BlockSpec((1, LANES), lambda i, h: (0, 0))],
        out_specs=pl.BlockSpec((s, hw), lambda i, h: (i, h)),
        out_shape=jax.ShapeDtypeStruct((b * s, BRANCH_WIDTH), BF16),
        compiler_params=_cparams("parallel", "arbitrary"),
        name="hgrn",
    )(p, p32, p, p, lb_logits, norm_g)


def _merge_kernel(xn_ref, ya_ref, yb_ref, yc_ref, yd_ref, wg_ref, bg_ref, wb_ref, o_ref):
    xn = xn_ref[...]
    acc = None
    for n, y_ref in enumerate((ya_ref, yb_ref, yc_ref, yd_ref)):
        gate = jax.nn.sigmoid(_dot(xn, wg_ref[n]) + bg_ref[n:n + 1, :])
        term = gate * _dot(y_ref[...], wb_ref[n])
        acc = term if acc is None else acc + term
    o_ref[...] = acc.astype(BF16)


def _merge(xn, ys, wg, bg, wb, tm, tn):
    t = xn.shape[0]
    yspec = pl.BlockSpec((tm, BRANCH_WIDTH), lambda i, j: (i, 0))
    return pl.pallas_call(
        _merge_kernel,
        grid=(t // tm, D_MODEL // tn),
        in_specs=[pl.BlockSpec((tm, D_MODEL), lambda i, j: (i, 0)), yspec, yspec, yspec, yspec,
                  pl.BlockSpec((N_BRANCH, D_MODEL, tn), lambda i, j: (0, 0, j)),
                  pl.BlockSpec((N_BRANCH, tn), lambda i, j: (0, j)),
                  pl.BlockSpec((N_BRANCH, BRANCH_WIDTH, tn), lambda i, j: (0, 0, j))],
        out_specs=pl.BlockSpec((tm, tn), lambda i, j: (i, j)),
        out_shape=jax.ShapeDtypeStruct((t, D_MODEL), BF16),
        compiler_params=_cparams("parallel", "arbitrary"),
        name="merge",
    )(xn, *ys, wg, bg, wb)


def _outproj_kernel(x_ref, m_ref, w_ref, o_ref):
    o_ref[...] = x_ref[...] + _dot(m_ref[...], w_ref[...])


def _outproj(x2, mixed, w, tm, tn):
    t = x2.shape[0]
    return pl.pallas_call(
        _outproj_kernel,
        grid=(t // tm, D_MODEL // tn),
        in_specs=[pl.BlockSpec((tm, tn), lambda i, j: (i, j)),
                  pl.BlockSpec((tm, D_MODEL), lambda i, j: (i, 0)),
                  pl.BlockSpec((D_MODEL, tn), lambda i, j: (0, j))],
        out_specs=pl.BlockSpec((tm, tn), lambda i, j: (i, j)),
        out_shape=jax.ShapeDtypeStruct((t, D_MODEL), F32),
        compiler_params=_cparams("parallel", "arbitrary"),
        name="outproj",
    )(x2, mixed, w)


CARRY_ROWS = 8
def _ffn_kernel(x_ref, g_ref, wa_ref, wg_ref, cwa_ref, cwg_ref, cba_ref, cbg_ref, wd_ref, gf_ref, o_ref,
                xn_s, carry_s, *, tiles_per_seq, final_norm):
    i = pl.program_id(0)
    j = pl.program_id(1)
    tm, tf = xn_s.shape[0], wa_ref.shape[1]

    @pl.when((i == 0) & (j == 0))
    def _():
        carry_s[...] = jnp.zeros_like(carry_s)

    @pl.when(j == 0)
    def _():
        x = x_ref[...]
        xn_s[...] = _rms(x, g_ref[...]).astype(BF16)
        o_ref[...] = x

    xn = xn_s[...]
    seq_start = (i % tiles_per_seq) == 0
    r8 = lax.broadcasted_iota(jnp.int32, (CARRY_ROWS, tf), 0)

    def conv(h, cw_ref, cb_ref, slot):
        prev = jnp.where(seq_start, 0.0, carry_s[slot])
        carry_s[slot] = h[tm - CARRY_ROWS:tm]
        h1 = pltpu.roll(h, 1, 0)
        h2 = pltpu.roll(h, 2, 0)
        p1 = pltpu.roll(prev, 1, 0)
        p2 = pltpu.roll(prev, 2, 0)
        h1 = jnp.concatenate([jnp.where(r8 < 1, p1, h1[:CARRY_ROWS]), h1[CARRY_ROWS:]], axis=0)
        h2 = jnp.concatenate([jnp.where(r8 < 2, p2, h2[:CARRY_ROWS]), h2[CARRY_ROWS:]], axis=0)
        cw = cw_ref[...]
        return cw[0:1] * h2 + cw[1:2] * h1 + cw[2:3] * h + cb_ref[...]

    ha = conv(_dot(xn, wa_ref[...]), cwa_ref, cba_ref, 2 * j)
    hg = conv(_dot(xn, wg_ref[...]), cwg_ref, cbg_ref, 2 * j + 1)
    act = (jax.nn.gelu(ha) * hg).astype(BF16)
    o_ref[...] += _dot(act, wd_ref[...])

    if final_norm:
        @pl.when(j == pl.num_programs(1) - 1)
        def _():
            o_ref[...] = _rms(o_ref[...], gf_ref[...])


def _ffn(x2, g, w_up, conv_w, conv_b, w_down, g_final, s, tm, tf, final_norm):
    t = x2.shape[0]
    nj = D_FF // tf
    return pl.pallas_call(
        functools.partial(_ffn_kernel, tiles_per_seq=s // tm, final_norm=final_norm),
        grid=(t // tm, nj),
        in_specs=[pl.BlockSpec((tm, D_MODEL), lambda i, j: (i, 0), pipeline_mode=pl.Buffered(1)),
                  pl.BlockSpec((1, D_MODEL), lambda i, j: (0, 0)),
                  pl.BlockSpec((D_MODEL, tf), lambda i, j: (0, j)),
                  pl.BlockSpec((D_MODEL, tf), lambda i, j: (0, j + nj)),
                  pl.BlockSpec((3, tf), lambda i, j: (0, j)),
                  pl.BlockSpec((3, tf), lambda i, j: (0, j + nj)),
                  pl.BlockSpec((1, tf), lambda i, j: (0, j)),
                  pl.BlockSpec((1, tf), lambda i, j: (0, j + nj)),
                  pl.BlockSpec((tf, D_MODEL), lambda i, j: (j, 0)),
                  pl.BlockSpec((1, D_MODEL), lambda i, j: (0, 0))],
        out_specs=pl.BlockSpec((tm, D_MODEL), lambda i, j: (i, 0)),
        out_shape=jax.ShapeDtypeStruct((t, D_MODEL), F32),
        scratch_shapes=[pltpu.VMEM((tm, D_MODEL), BF16), pltpu.VMEM((2 * nj, CARRY_ROWS, tf), F32)],
        compiler_params=_cparams("arbitrary", "arbitrary"),
        name="ffn",
    )(x2, g, w_up, w_up, conv_w, conv_w, conv_b, conv_b, w_down, g_final)


def _pad_w_in(w):
    bw = BRANCH_WIDTH
    a, bqkv, bz, cq, cz, ci, cg, dqk, dv = jnp.split(
        w, [2 * bw, 5 * bw, 5 * bw + B_HEADS, 6 * bw + B_HEADS, 7 * bw + B_HEADS, 8 * bw + B_HEADS,
            9 * bw + B_HEADS, 11 * bw + B_HEADS], axis=1)
    z1 = jnp.zeros((D_MODEL, LANES - B_HEADS), w.dtype)
    z2 = jnp.zeros((D_MODEL, LANES), w.dtype)
    out = jnp.concatenate([a, bqkv, cq, ci, cg, dqk, cz, bz, z1, dv, z2], axis=1).astype(BF16)
    assert out.shape[1] == IN_COLS_PAD
    return out


def _rope_tables(s):
    half = D_HEAD // 2
    inv_freq = ROPE_THETA ** (-jnp.arange(half, dtype=F32) / half)
    ang = jnp.arange(s, dtype=jnp.int32).astype(F32)[:, None] * inv_freq[None, :]
    cos, sin = jnp.cos(ang), jnp.sin(ang)
    cos_full = jnp.tile(cos, (1, LANES // half))
    sin_signed = jnp.tile(jnp.concatenate([-sin, sin], axis=1), (1, LANES // D_HEAD))
    return cos_full, sin_signed


def _tiles(b, s):
    return {"attn_q": min(512, s), "attn_k": min(512, s), "inproj": min(1024, b * s), "merge": min(1024, b * s),
            "ffn": min(1024, s), "hgrn_heads": 4}


def kernel(x, norm_mix_g, w_in, fox_b_f, gmlp_ln_g, gmlp_ln_b, gmlp_w_s, gmlp_b_s, hgrn_lb_logits, hgrn_norm_g, diff_lambda, diff_norm_g, w_branch, w_gate, b_gate, w_out, norm_ffn_g, ffn_w_up, ffn_conv_w, ffn_conv_b, ffn_w_down, norm_final_g):
    b, s, _ = x.shape
    t = b * s
    assert s % GMLP_BLOCK == 0
    tl = _tiles(b, s)
    tq, tk, tm_in, tm_merge, tm_ffn = tl["attn_q"], tl["attn_k"], tl["inproj"], tl["merge"], tl["ffn"]
    assert tk % tq == 0 and tq % CHUNK == 0
    x2 = x.reshape(t, D_MODEL)
    cos_full, sin_signed = _rope_tables(s)
    bf_pad = jnp.pad(fox_b_f, ((0, 0), (0, LANES - B_HEADS)))
    for l in range(DEPTH):
        lam_init = 0.8 - 0.6 * math.exp(-0.3 * l)
        p, p32, xn = _rms_inproj(x2, norm_mix_g[l][None], _pad_w_in(w_in[l]), tm_in, IN_TILE)
        y_a = _gmlp(p, gmlp_ln_g[l][None], gmlp_ln_b[l][None], gmlp_w_s[l], gmlp_b_s[l].T, min(256, s))
        c, ct = _foxgate(p32, bf_pad[l][None], b, s, tk)
        y_b = _fox(p, c, ct, b, s, tq, tk)
        y_c = _hgrn(p, p32, hgrn_lb_logits, hgrn_norm_g[l][None], b, s, l, tl["hgrn_heads"])
        y_d = _diff(p, cos_full, sin_signed, diff_lambda[l], diff_norm_g[l][None], b, s, tq, tk, lam_init)
        mixed = _merge(xn, (y_a, y_b, y_c, y_d), w_gate[l].astype(BF16), b_gate[l], w_branch[l].astype(BF16),
                       tm_merge, 512)
        x2 = _outproj(x2, mixed, w_out[l].astype(BF16), tm_merge, 1024)
        x2 = _ffn(x2, norm_ffn_g[l][None], ffn_w_up[l].astype(BF16), ffn_conv_w[l], ffn_conv_b[l][None],
                  ffn_w_down[l].astype(BF16), norm_final_g[None], s, tm_ffn, 512, l == DEPTH - 1)
    return x2.reshape(b, s, D_MODEL)
```

```python
import functools
import math

import jax
import jax.numpy as jnp
from jax import lax
from jax.experimental import pallas as pl
from jax.experimental.pallas import tpu as pltpu

F32 = jnp.float32
BF16 = jnp.bfloat16

D_MODEL = 2048
DEPTH = 2
CHUNK = 64
BRANCH_WIDTH = 512
N_BRANCH = 4
GMLP_BLOCK = 128
A_GROUPS = 4
B_HEADS = 8
C_KDIM = 128
D_HEAD = 64
D_FF = 5632
ROPE_THETA = 10000.0
EPS = 1e-6
LANES = 128
LOG2E = 1.4426950408889634

IN_COLS_PAD = 6400
IN_TILE = 1280
COL_A = 0
COL_BQ, COL_BK, COL_BV = 8, 12, 16
COL_CQ, COL_CI, COL_CG = 20, 24, 28
COL_DQ, COL_DK = 32, 36
COL_CZ, COL_BZ, COL_DV = 40, 44, 45
COL32_CZ, COL32_BZ = COL_CZ - 40, COL_BZ - 40

VMEM_LIMIT = 56 * 1024 * 1024


def _cparams(*sem):
    return pltpu.CompilerParams(dimension_semantics=sem, vmem_limit_bytes=VMEM_LIMIT)


def _rms(x, g):
    ms = jnp.mean(x * x, axis=-1, keepdims=True)
    return x * lax.rsqrt(ms + EPS) * g


def _dot(a, b):
    return jnp.dot(a, b, preferred_element_type=F32)


def _dot_nt(a, b):
    return lax.dot_general(a, b, (((1,), (1,)), ((), ())), preferred_element_type=F32)


def _dot_tn(a, b):
    return lax.dot_general(a, b, (((0,), (0,)), ((), ())), preferred_element_type=F32)


def _tri_cumsum(tri, x):
    hi = x.astype(BF16)
    r1 = x - hi.astype(F32)
    mid = r1.astype(BF16)
    lo = (r1 - mid.astype(F32)).astype(BF16)
    return (_dot(tri, hi) + _dot(tri, mid)) + _dot(tri, lo)


def _lower_tri(n):
    return (lax.broadcasted_iota(jnp.int32, (n, n), 0) >= lax.broadcasted_iota(jnp.int32, (n, n), 1)).astype(BF16)


def _cast_kernel(w_ref, o_ref):
    o_ref[...] = w_ref[...].astype(BF16)


def _cast_bf16(w, layer, br, bc):
    _, r, c = w.shape
    return pl.pallas_call(
        _cast_kernel,
        grid=(r // br, c // bc),
        in_specs=[pl.BlockSpec((None, br, bc), lambda i, j: (layer, i, j))],
        out_specs=pl.BlockSpec((br, bc), lambda i, j: (i, j)),
        out_shape=jax.ShapeDtypeStruct((r, c), BF16),
        compiler_params=_cparams("parallel", "parallel"),
        name="cast_bf16",
    )(w)


def _rms_inproj_kernel(x_ref, g_ref, w_ref, p_ref, p32_ref, xn_ref):
    @pl.when(pl.program_id(1) == 0)
    def _():
        xn_ref[...] = _rms(x_ref[...], g_ref[...]).astype(BF16)

    acc = _dot(xn_ref[...], w_ref[...])
    p_ref[...] = acc.astype(BF16)

    @pl.when(pl.program_id(1) == pl.num_programs(1) - 1)
    def _():
        p32_ref[...] = acc


def _rms_inproj(x2, g, w, tm, tn):
    t, np_ = x2.shape[0], w.shape[1]
    return pl.pallas_call(
        _rms_inproj_kernel,
        grid=(t // tm, np_ // tn),
        in_specs=[pl.BlockSpec((tm, D_MODEL), lambda i, j: (i, 0)),
                  pl.BlockSpec((1, D_MODEL), lambda i, j: (0, 0)),
                  pl.BlockSpec((D_MODEL, tn), lambda i, j: (0, j))],
        out_specs=[pl.BlockSpec((tm, tn), lambda i, j: (i, j)),
                   pl.BlockSpec((tm, tn), lambda i, j: (i, 0)),
                   pl.BlockSpec((tm, D_MODEL), lambda i, j: (i, 0))],
        out_shape=[jax.ShapeDtypeStruct((t, np_), BF16), jax.ShapeDtypeStruct((t, tn), F32),
                   jax.ShapeDtypeStruct((t, D_MODEL), BF16)],
        compiler_params=_cparams("parallel", "arbitrary"),
        name="rms_inproj",
    )(x2, g, w)


def _gmlp_kernel(p_ref, lng_ref, lnb_ref, ws_ref, bs_ref, y_ref):
    ta = p_ref.shape[0]
    pa = jax.nn.gelu(p_ref[...].astype(F32))
    u = pa[:, :BRANCH_WIDTH]
    v = pa[:, BRANCH_WIDTH:]
    mu = jnp.mean(v, axis=-1, keepdims=True)
    d = v - mu
    var = jnp.mean(d * d, axis=-1, keepdims=True)
    vn = (d * lax.rsqrt(var + EPS) * lng_ref[...] + lnb_ref[...]).astype(BF16)
    rc = lax.broadcasted_iota(jnp.int32, (GMLP_BLOCK, GMLP_BLOCK), 0) // CHUNK
    cc = lax.broadcasted_iota(jnp.int32, (GMLP_BLOCK, GMLP_BLOCK), 1) // CHUNK
    mask = rc >= cc
    for g in range(A_GROUPS):
        wg = jnp.where(mask, ws_ref[g], 0.0).astype(BF16)
        bcol = bs_ref[:, g:g + 1]
        cs = slice(g * LANES, (g + 1) * LANES)
        for blk in range(ta // GMLP_BLOCK):
            rs = slice(blk * GMLP_BLOCK, (blk + 1) * GMLP_BLOCK)
            mixed = _dot(wg, vn[rs, cs]) + bcol
            y_ref[rs, cs] = (u[rs, cs] * mixed).astype(BF16)


def _gmlp(p, ln_g, ln_b, w_s, b_s_t, ta):
    t = p.shape[0]
    return pl.pallas_call(
        _gmlp_kernel,
        grid=(t // ta,),
        in_specs=[pl.BlockSpec((ta, 2 * BRANCH_WIDTH), lambda i: (i, 0)),
                  pl.BlockSpec((1, BRANCH_WIDTH), lambda i: (0, 0)),
                  pl.BlockSpec((1, BRANCH_WIDTH), lambda i: (0, 0)),
                  pl.BlockSpec((A_GROUPS, GMLP_BLOCK, GMLP_BLOCK), lambda i: (0, 0, 0)),
                  pl.BlockSpec((GMLP_BLOCK, A_GROUPS), lambda i: (0, 0))],
        out_specs=pl.BlockSpec((ta, BRANCH_WIDTH), lambda i: (i, 0)),
        out_shape=jax.ShapeDtypeStruct((t, BRANCH_WIDTH), BF16),
        compiler_params=_cparams("parallel"),
        name="gmlp",
    )(p, ln_g, ln_b, w_s, b_s_t)


def _foxgate_kernel(p_ref, bf_ref, c_ref, ct_ref, *, tb):
    s = p_ref.shape[0]
    seg = 256 if s % 256 == 0 else s
    z = p_ref[...] + bf_ref[...]
    ls = jnp.minimum(z, 0.0) - jnp.log1p(jnp.exp(-jnp.abs(z)))
    tri = _lower_tri(seg)
    carry = jnp.zeros((1, LANES), F32)
    for blk in range(s // seg):
        rs = slice(blk * seg, (blk + 1) * seg)
        cs = _tri_cumsum(tri, ls[rs]) + carry
        c_ref[rs, :] = cs * LOG2E
        carry = cs[seg - 1:seg, :]
    ct = c_ref[...].T
    for j in range(s // tb):
        ct_ref[0, j] = ct[:B_HEADS, j * tb:(j + 1) * tb]


def _foxgate(p, bf, b, s, tb):
    return pl.pallas_call(
        functools.partial(_foxgate_kernel, tb=tb),
        grid=(b,),
        in_specs=[pl.BlockSpec((s, LANES), lambda i: (i, COL32_BZ)),
                  pl.BlockSpec((1, LANES), lambda i: (0, 0))],
        out_specs=[pl.BlockSpec((s, LANES), lambda i: (i, 0)),
                   pl.BlockSpec((1, s // tb, B_HEADS, tb), lambda i: (i, 0, 0, 0))],
        out_shape=[jax.ShapeDtypeStruct((b * s, LANES), F32),
                   jax.ShapeDtypeStruct((b, s // tb, B_HEADS, tb), F32)],
        compiler_params=_cparams("parallel"),
        name="foxgate",
    )(p, bf)


def _flash(streams, kt_ref, vb_ref, n_full, tk, diag_mask):
    tq = streams[0][0].shape[0]

    def step(j, carry, masked):
        ktj = kt_ref[j]
        vj = vb_ref[pl.ds(pl.multiple_of(j * tk, tk), tk), :]
        new = []
        scs = [_dot(qh, ktj) for qh, _ in streams]
        for sc, (_, bias_fn), (m, l, acc) in zip(scs, streams, carry):
            if bias_fn is not None:
                sc = bias_fn(sc, j)
            if masked:
                sc = jnp.where(diag_mask, sc, -jnp.inf)
            m_new = jnp.maximum(m, jnp.max(sc, axis=-1, keepdims=True))
            alpha = jnp.exp2(m - m_new)
            pr = jnp.exp2(sc - m_new)
            l = alpha * l + jnp.sum(pr, axis=-1, keepdims=True)
            acc = alpha * acc + _dot(pr.astype(BF16), vj)
            new.append((m_new, l, acc))
        return tuple(new)

    init = tuple((jnp.full((tq, 1), -jnp.inf, F32), jnp.zeros((tq, 1), F32), jnp.zeros((tq, LANES), F32))
                 for _ in streams)
    carry = lax.fori_loop(0, n_full, lambda j, c: step(j, c, False), init)
    return [acc / l for _, l, acc in step(n_full, carry, True)]


def _flash_t(streams, kb_ref, vt_ref, n_full, tk, diag_mask):
    tq = streams[0][0].shape[1]

    def step(j, carry, masked):
        kj = kb_ref[pl.ds(pl.multiple_of(j * tk, tk), tk), :]
        vtj = vt_ref[j]
        new = []
        scs = [_dot(kj, qt) for qt, _ in streams]
        for sc, (_, bias_fn), (m, l, acc) in zip(scs, streams, carry):
            if bias_fn is not None:
                sc = bias_fn(sc, j)
            if masked:
                sc = jnp.where(diag_mask, sc, -jnp.inf)
            m_new = jnp.maximum(m, jnp.max(sc, axis=0, keepdims=True))
            alpha = jnp.exp2(m - m_new)
            pr = jnp.exp2(sc - m_new)
            l = alpha * l + jnp.sum(pr, axis=0, keepdims=True)
            acc = alpha * acc + _dot(vtj, pr.astype(BF16))
            new.append((m_new, l, acc))
        return tuple(new)

    init = tuple((jnp.full((1, tq), -jnp.inf, F32), jnp.zeros((1, tq), F32), jnp.zeros((LANES, tq), F32))
                 for _ in streams)
    carry = lax.fori_loop(0, n_full, lambda j, c: step(j, c, False), init)
    return [acc / l for _, l, acc in step(n_full, carry, True)]


def _diag_offsets(qi, tq, tk):
    jd = (qi * tq) // tk
    return jd, jd * tk - qi * tq


def _fox_kernel(q_ref, k_ref, v_ref, c_ref, ct_ref, o_ref, vt_s, kc_s, *, tq, tk):
    s = q_ref.shape[0]
    hp = pl.program_id(1)
    lane = lax.broadcasted_iota(jnp.int32, (1, LANES), 1)
    for j in range(s // tk):
        rs = slice(j * tk, (j + 1) * tk)
        vt_s[j] = v_ref[rs, :].astype(F32).T.astype(BF16)
        for hh in range(2):
            col = jnp.sum(jnp.where(lane == 2 * hp + hh, c_ref[rs, :], 0.0), axis=-1, keepdims=True)
            kc_s[hh, rs, :] = jnp.broadcast_to(col, (tk, LANES))
    col_minus_row = lax.broadcasted_iota(jnp.int32, (tk, tq), 1) - lax.broadcasted_iota(jnp.int32, (tk, tq), 0)
    feat = lax.broadcasted_iota(jnp.int32, (LANES, 1), 0)
    scale = (D_HEAD ** -0.5) * LOG2E

    def qblock(qi, _):
        rs = pl.ds(pl.multiple_of(qi * tq, tq), tq)
        qt = (q_ref[rs, :].astype(F32) * scale).T
        jd, off = _diag_offsets(qi, tq, tk)
        streams = []
        for hh in range(2):
            qh = jnp.where((feat // D_HEAD) == hh, qt, 0.0).astype(BF16)
            crow = ct_ref[0, qi, pl.ds(2 * hp + hh, 1), :]

            def bias(sc, j, hh=hh, crow=crow):
                kc = kc_s[hh, pl.ds(pl.multiple_of(j * tk, tk), tk), :]
                return sc + jnp.concatenate(
                    [crow[:, c0:c0 + LANES] - kc for c0 in range(0, tq, LANES)], axis=1)

            streams.append((qh, bias))
        o0, o1 = _flash_t(streams, k_ref, vt_s, jd, tk, col_minus_row >= off)
        o_ref[rs, :] = jnp.where(feat < D_HEAD, o0, o1).T.astype(BF16)
        return 0

    lax.fori_loop(0, s // tq, qblock, 0)


def _fox(p, c, ct, b, s, tq, tk):
    nk = s // tk
    return pl.pallas_call(
        functools.partial(_fox_kernel, tq=tq, tk=tk),
        grid=(b, B_HEADS // 2),
        in_specs=[pl.BlockSpec((s, LANES), lambda i, h: (i, COL_BQ + h)),
                  pl.BlockSpec((s, LANES), lambda i, h: (i, COL_BK + h)),
                  pl.BlockSpec((s, LANES), lambda i, h: (i, COL_BV + h)),
                  pl.BlockSpec((s, LANES), lambda i, h: (i, 0)),
                  pl.BlockSpec((1, nk, B_HEADS, tk), lambda i, h: (i, 0, 0, 0))],
        out_specs=pl.BlockSpec((s, LANES), lambda i, h: (i, h)),
        out_shape=jax.ShapeDtypeStruct((b * s, BRANCH_WIDTH), BF16),
        scratch_shapes=[pltpu.VMEM((nk, LANES, tk), BF16), pltpu.VMEM((2, s, LANES), F32)],
        compiler_params=_cparams("parallel", "arbitrary"),
        name="fox_attn",
    )(p, p, p, c, ct)


def _diff_kernel(q_ref, k_ref, v_ref, cos_ref, sin_ref, lam_ref, g_ref, o_ref, kb_s, vt_s, *, tq, tk, lam_init):
    s = q_ref.shape[0]
    lane = lax.broadcasted_iota(jnp.int32, (1, LANES), 1)
    lower_half = (lane % D_HEAD) < (D_HEAD // 2)

    def rope(x, cos, sin_signed):
        partner = jnp.where(lower_half, pltpu.roll(x, LANES - D_HEAD // 2, 1), pltpu.roll(x, D_HEAD // 2, 1))
        return x * cos + partner * sin_signed

    for j in range(s // tk):
        rs = slice(j * tk, (j + 1) * tk)
        kb_s[rs, :] = rope(k_ref[rs, :].astype(F32), cos_ref[rs, :], sin_ref[rs, :]).astype(BF16)
        vt_s[j] = v_ref[rs, :].astype(F32).T.astype(BF16)

    lp = lam_ref[...]
    lam = (jnp.exp(jnp.sum(lp[0:1] * lp[1:2], axis=-1, keepdims=True))
           - jnp.exp(jnp.sum(lp[2:3] * lp[3:4], axis=-1, keepdims=True)) + lam_init)
    chunk_col_minus_row = (lax.broadcasted_iota(jnp.int32, (tk, tq), 1) // CHUNK
                           - lax.broadcasted_iota(jnp.int32, (tk, tq), 0) // CHUNK)
    feat = lax.broadcasted_iota(jnp.int32, (LANES, 1), 0)
    scale = (D_HEAD ** -0.5) * LOG2E

    def qblock(qi, _):
        rs = pl.ds(pl.multiple_of(qi * tq, tq), tq)
        qt = (rope(q_ref[rs, :].astype(F32), cos_ref[rs, :], sin_ref[rs, :]) * scale).T
        q1 = jnp.where(feat < D_HEAD, qt, 0.0).astype(BF16)
        q2 = jnp.where(feat >= D_HEAD, qt, 0.0).astype(BF16)
        jd, off = _diag_offsets(qi, tq, tk)
        o1, o2 = _flash_t([(q1, None), (q2, None)], kb_s, vt_s, jd, tk, chunk_col_minus_row >= off // CHUNK)
        o = (o1 - lam * o2).T
        o_ref[rs, :] = (_rms(o, g_ref[...]) * (1.0 - lam_init)).astype(BF16)
        return 0

    lax.fori_loop(0, s // tq, qblock, 0)


def _diff(p, cos, sin_signed, lam_p, norm_g, b, s, tq, tk, lam_init):
    nb = s // tk
    return pl.pallas_call(
        functools.partial(_diff_kernel, tq=tq, tk=tk, lam_init=lam_init),
        grid=(b, BRANCH_WIDTH // LANES),
        in_specs=[pl.BlockSpec((s, LANES), lambda i, h: (i, COL_DQ + h)),
                  pl.BlockSpec((s, LANES), lambda i, h: (i, COL_DK + h)),
                  pl.BlockSpec((s, LANES), lambda i, h: (i, COL_DV + h)),
                  pl.BlockSpec((s, LANES), lambda i, h: (0, 0)),
                  pl.BlockSpec((s, LANES), lambda i, h: (0, 0)),
                  pl.BlockSpec((4, D_HEAD), lambda i, h: (0, 0)),
                  pl.BlockSpec((1, LANES), lambda i, h: (0, 0))],
        out_specs=pl.BlockSpec((s, LANES), lambda i, h: (i, h)),
        out_shape=jax.ShapeDtypeStruct((b * s, BRANCH_WIDTH), BF16),
        scratch_shapes=[pltpu.VMEM((s, LANES), BF16), pltpu.VMEM((nb, LANES, tk), BF16)],
        compiler_params=_cparams("parallel", "arbitrary"),
        name="diff_attn",
    )(p, p, p, cos, sin_signed, lam_p, norm_g)


SUB = 16
HALF = 8


def _hgrn_kernel(q_ref, z_ref, i_ref, g_ref, lbl_ref, ng_ref, o_ref, *, layer):
    s = q_ref.shape[0]
    heads = q_ref.shape[1] // LANES
    lbl = lbl_ref[...]
    e = jnp.exp(lbl - jnp.max(lbl, axis=0, keepdims=True))
    pr = e / jnp.sum(e, axis=0, keepdims=True)
    lb_all = pr[0:1] - pr[0:1]
    for r in range(1, layer + 1):
        lb_all = lb_all + pr[r:r + 1]
    tri = _lower_tri(CHUNK)
    lane_s = lax.broadcasted_iota(jnp.int32, (SUB, CHUNK), 1)
    lane_h = lax.broadcasted_iota(jnp.int32, (HALF, CHUNK), 1)
    row_h = lax.broadcasted_iota(jnp.int32, (HALF, CHUNK), 0)
    qscale = C_KDIM ** -0.5

    def head_chunk(rs, cs, st):
        lb = lb_all[:, cs]
        oml = 1.0 - lb
        z = z_ref[rs, cs]
        f = lb + oml * jax.nn.sigmoid(z)
        kk = oml * jax.nn.sigmoid(-z)
        cum = _tri_cumsum(tri, jnp.log(f)) * LOG2E
        q = q_ref[rs, cs].astype(F32) * qscale
        vb = i_ref[rs, cs].astype(BF16)
        inter = _dot_nt((q * jnp.exp2(cum)).astype(BF16), st.astype(BF16))
        blocks = []
        for bi in range(CHUNK // SUB):
            b0 = bi * SUB
            if bi == 0:
                halves = [jnp.zeros((HALF, CHUNK), F32)] * 2
            else:
                ref_row = cum[b0 - 1:b0]
                qi_ = (q[b0:b0 + SUB] * jnp.exp2(cum[b0:b0 + SUB] - ref_row)).astype(BF16)
                ki_ = (kk * jnp.exp2(jnp.minimum(ref_row - cum, 0.0))).astype(BF16)
                sc = jnp.where(lane_s < b0, _dot_nt(qi_, ki_), 0.0)
                halves = [sc[:HALF], sc[HALF:]]
            for si in range(SUB):
                c_s, k_s = cum[b0 + si:b0 + si + 1], kk[b0 + si:b0 + si + 1]
                for hf in range(si // HALF, 2):
                    r0 = b0 + hf * HALF
                    x = q[r0:r0 + HALF] * jnp.exp2(cum[r0:r0 + HALF] - c_s) * k_s
                    col = jnp.sum(x, axis=-1, keepdims=True)
                    halves[hf] = jnp.where(lane_h == b0 + si, col, halves[hf])
            for hf in range(2):
                blocks.append(jnp.where(lane_h <= row_h + (b0 + hf * HALF), halves[hf], 0.0))
        scores = jnp.concatenate(blocks, axis=0)
        out = inter + _dot(scores.astype(BF16), vb)
        last = cum[CHUNK - 1:CHUNK]
        kdec = (kk * jnp.exp2(last - cum)).astype(BF16)
        st_new = st * jnp.exp2(last) + _dot_tn(vb, kdec)
        y = _rms(out, ng_ref[...]) * jax.nn.sigmoid(g_ref[rs, cs].astype(F32))
        o_ref[rs, cs] = y.astype(BF16)
        return st_new

    def chunk_body(n, states):
        rs = pl.ds(pl.multiple_of(n * CHUNK, CHUNK), CHUNK)
        return tuple(head_chunk(rs, slice(hd * LANES, (hd + 1) * LANES), st) for hd, st in enumerate(states))

    lax.fori_loop(0, s // CHUNK, chunk_body, tuple(jnp.zeros((LANES, C_KDIM), F32) for _ in range(heads)))


def _hgrn(p, p32, lb_logits, norm_g, b, s, layer, heads):
    hw = heads * LANES
    return pl.pallas_call(
        functools.partial(_hgrn_kernel, layer=layer),
        grid=(b, BRANCH_WIDTH // hw),
        in_specs=[pl.BlockSpec((s, hw), lambda i, h: (i, COL_CQ // heads + h)),
                  pl.BlockSpec((s, hw), lambda i, h: (i, COL32_CZ // heads + h)),
                  pl.BlockSpec((s, hw), lambda i, h: (i, COL_CI // heads + h)),
                  pl.BlockSpec((s, hw), lambda i, h: (i, COL_CG // heads + h)),
                  pl.BlockSpec((DEPTH, hw), lambda i, h: (0, h)),
                  pl.BlockSpec((1, LANES), lambda i, h: (0, 0))],
        out_specs=pl.BlockSpec((s, hw), lambda i, h: (i, h)),
        out_shape=jax.ShapeDtypeStruct((b * s, BRANCH_WIDTH), BF16),
        compiler_params=_cparams("parallel", "arbitrary"),
        name="hgrn",
    )(p, p32, p, p, lb_logits, norm_g)


def _merge_kernel(xn_ref, ya_ref, yb_ref, yc_ref, yd_ref, wg_ref, bg_ref, wb_ref, o_ref):
    xn = xn_ref[...]
    acc = None
    for n, y_ref in enumerate((ya_ref, yb_ref, yc_ref, yd_ref)):
        gate = jax.nn.sigmoid(_dot(xn, wg_ref[n]) + bg_ref[n:n + 1, :])
        term = gate * _dot(y_ref[...], wb_ref[n])
        acc = term if acc is None else acc + term
    o_ref[...] = acc.astype(BF16)


def _merge(xn, ys, wg, bg, wb, tm, tn):
    t = xn.shape[0]
    yspec = pl.BlockSpec((tm, BRANCH_WIDTH), lambda i, j: (i, 0))
    return pl.pallas_call(
        _merge_kernel,
        grid=(t // tm, D_MODEL // tn),
        in_specs=[pl.BlockSpec((tm, D_MODEL), lambda i, j: (i, 0)), yspec, yspec, yspec, yspec,
                  pl.BlockSpec((N_BRANCH, D_MODEL, tn), lambda i, j: (0, 0, j)),
                  pl.BlockSpec((N_BRANCH, tn), lambda i, j: (0, j)),
                  pl.BlockSpec((N_BRANCH, BRANCH_WIDTH, tn), lambda i, j: (0, 0, j))],
        out_specs=pl.BlockSpec((tm, tn), lambda i, j: (i, j)),
        out_shape=jax.ShapeDtypeStruct((t, D_MODEL), BF16),
        compiler_params=_cparams("parallel", "arbitrary"),
        name="merge",
    )(xn, *ys, wg, bg, wb)


def _outproj_kernel(x_ref, m_ref, w_ref, o_ref):
    o_ref[...] = x_ref[...] + _dot(m_ref[...], w_ref[...])


def _outproj(x2, mixed, w, tm, tn):
    t = x2.shape[0]
    return pl.pallas_call(
        _outproj_kernel,
        grid=(t // tm, D_MODEL // tn),
        in_specs=[pl.BlockSpec((tm, tn), lambda i, j: (i, j)),
                  pl.BlockSpec((tm, D_MODEL), lambda i, j: (i, 0)),
                  pl.BlockSpec((D_MODEL, tn), lambda i, j: (0, j))],
        out_specs=pl.BlockSpec((tm, tn), lambda i, j: (i, j)),
        out_shape=jax.ShapeDtypeStruct((t, D_MODEL), F32),
        compiler_params=_cparams("parallel", "arbitrary"),
        name="outproj",
    )(x2, mixed, w)


CARRY_ROWS = 8


def _ffn_kernel(x_ref, g_ref, wa_ref, wg_ref, cwa_ref, cwg_ref, cba_ref, cbg_ref, wd_ref, gf_ref, o_ref,
                xn_s, carry_s, *, tiles_per_seq, final_norm):
    i = pl.program_id(0)
    j = pl.program_id(1)
    tm, tf = xn_s.shape[0], wa_ref.shape[1]

    @pl.when((i == 0) & (j == 0))
    def _():
        carry_s[...] = jnp.zeros_like(carry_s)

    @pl.when(j == 0)
    def _():
        x = x_ref[...]
        xn_s[...] = _rms(x, g_ref[...]).astype(BF16)
        o_ref[...] = x

    seq_start = (i % tiles_per_seq) == 0
    r8 = lax.broadcasted_iota(jnp.int32, (CARRY_ROWS, tf), 0)

    def conv(h, cw_ref, cb_ref, slot):
        prev = jnp.where(seq_start, 0.0, carry_s[slot])
        carry_s[slot] = h[tm - CARRY_ROWS:tm]
        h1 = pltpu.roll(h, 1, 0)
        h2 = pltpu.roll(h, 2, 0)
        p1 = pltpu.roll(prev, 1, 0)
        p2 = pltpu.roll(prev, 2, 0)
        h1 = jnp.concatenate([jnp.where(r8 < 1, p1, h1[:CARRY_ROWS]), h1[CARRY_ROWS:]], axis=0)
        h2 = jnp.concatenate([jnp.where(r8 < 2, p2, h2[:CARRY_ROWS]), h2[CARRY_ROWS:]], axis=0)
        cw = cw_ref[...]
        return cw[0:1] * h2 + cw[1:2] * h1 + cw[2:3] * h + cb_ref[...]

    xn = xn_s[...]
    ha = conv(_dot(xn, wa_ref[...]), cwa_ref, cba_ref, 2 * j)
    hg = conv(_dot(xn, wg_ref[...]), cwg_ref, cbg_ref, 2 * j + 1)
    act = (jax.nn.gelu(ha) * hg).astype(BF16)
    o_ref[...] += _dot(act, wd_ref[...])

    if final_norm:
        @pl.when(j == pl.num_programs(1) - 1)
        def _():
            o_ref[...] = _rms(o_ref[...], gf_ref[...])


def _ffn(x2, g, w_up, conv_w, conv_b, w_down, g_final, s, tm, tf, final_norm):
    t = x2.shape[0]
    nj = D_FF // tf
    return pl.pallas_call(
        functools.partial(_ffn_kernel, tiles_per_seq=s // tm, final_norm=final_norm),
        grid=(t // tm, nj),
        in_specs=[pl.BlockSpec((tm, D_MODEL), lambda i, j: (i, 0), pipeline_mode=pl.Buffered(1)),
                  pl.BlockSpec((1, D_MODEL), lambda i, j: (0, 0)),
                  pl.BlockSpec((D_MODEL, tf), lambda i, j: (0, j)),
                  pl.BlockSpec((D_MODEL, tf), lambda i, j: (0, j + nj)),
                  pl.BlockSpec((3, tf), lambda i, j: (0, j)),
                  pl.BlockSpec((3, tf), lambda i, j: (0, j + nj)),
                  pl.BlockSpec((1, tf), lambda i, j: (0, j)),
                  pl.BlockSpec((1, tf), lambda i, j: (0, j + nj)),
                  pl.BlockSpec((tf, D_MODEL), lambda i, j: (j, 0)),
                  pl.BlockSpec((1, D_MODEL), lambda i, j: (0, 0))],
        out_specs=pl.BlockSpec((tm, D_MODEL), lambda i, j: (i, 0)),
        out_shape=jax.ShapeDtypeStruct((t, D_MODEL), F32),
        scratch_shapes=[pltpu.VMEM((tm, D_MODEL), BF16), pltpu.VMEM((2 * nj, CARRY_ROWS, tf), F32)],
        compiler_params=_cparams("arbitrary", "arbitrary"),
        name="ffn",
    )(x2, g, w_up, w_up, conv_w, conv_w, conv_b, conv_b, w_down, g_final)


def _pad_w_in(w):
    bw = BRANCH_WIDTH
    a, bqkv, bz, cq, cz, ci, cg, dqk, dv = jnp.split(
        w, [2 * bw, 5 * bw, 5 * bw + B_HEADS, 6 * bw + B_HEADS, 7 * bw + B_HEADS, 8 * bw + B_HEADS,
            9 * bw + B_HEADS, 11 * bw + B_HEADS], axis=1)
    z1 = jnp.zeros((D_MODEL, LANES - B_HEADS), w.dtype)
    z2 = jnp.zeros((D_MODEL, LANES), w.dtype)
    out = jnp.concatenate([a, bqkv, cq, ci, cg, dqk, cz, bz, z1, dv, z2], axis=1).astype(BF16)
    assert out.shape[1] == IN_COLS_PAD
    return out


def _rope_tables(s):
    half = D_HEAD // 2
    inv_freq = ROPE_THETA ** (-jnp.arange(half, dtype=F32) / half)
    ang = jnp.arange(s, dtype=jnp.int32).astype(F32)[:, None] * inv_freq[None, :]
    cos, sin = jnp.cos(ang), jnp.sin(ang)
    cos_full = jnp.tile(cos, (1, LANES // half))
    sin_signed = jnp.tile(jnp.concatenate([-sin, sin], axis=1), (1, LANES // D_HEAD))
    return cos_full, sin_signed


def _tiles(b, s):
    return {"attn_q": min(512, s), "attn_k": min(512, s), "inproj": min(1024, b * s), "merge": min(1024, b * s),
            "ffn": min(1024, s), "hgrn_heads": 4}


def kernel(x, norm_mix_g, w_in, fox_b_f, gmlp_ln_g, gmlp_ln_b, gmlp_w_s, gmlp_b_s, hgrn_lb_logits, hgrn_norm_g, diff_lambda, diff_norm_g, w_branch, w_gate, b_gate, w_out, norm_ffn_g, ffn_w_up, ffn_conv_w, ffn_conv_b, ffn_w_down, norm_final_g):
    b, s, _ = x.shape
    t = b * s
    assert s % GMLP_BLOCK == 0
    tl = _tiles(b, s)
    tq, tk, tm_in, tm_merge, tm_ffn = tl["attn_q"], tl["attn_k"], tl["inproj"], tl["merge"], tl["ffn"]
    assert tq == tk and tq % CHUNK == 0
    x2 = x.reshape(t, D_MODEL)
    cos_full, sin_signed = _rope_tables(s)
    bf_pad = jnp.pad(fox_b_f, ((0, 0), (0, LANES - B_HEADS)))
    for l in range(DEPTH):
        lam_init = 0.8 - 0.6 * math.exp(-0.3 * l)
        p, p32, xn = _rms_inproj(x2, norm_mix_g[l][None], _pad_w_in(w_in[l]), tm_in, IN_TILE)
        y_a = _gmlp(p, gmlp_ln_g[l][None], gmlp_ln_b[l][None], gmlp_w_s[l], gmlp_b_s[l].T, min(256, s))
        c, ct = _foxgate(p32, bf_pad[l][None], b, s, tk)
        y_b = _fox(p, c, ct, b, s, tq, tk)
        y_c = _hgrn(p, p32, hgrn_lb_logits, hgrn_norm_g[l][None], b, s, l, tl["hgrn_heads"])
        y_d = _diff(p, cos_full, sin_signed, diff_lambda[l], diff_norm_g[l][None], b, s, tq, tk, lam_init)
        wg = _cast_bf16(w_gate.reshape(DEPTH, N_BRANCH * D_MODEL, D_MODEL), l, 1024, D_MODEL)
        wb = _cast_bf16(w_branch.reshape(DEPTH, N_BRANCH * BRANCH_WIDTH, D_MODEL), l, 1024, D_MODEL)
        mixed = _merge(xn, (y_a, y_b, y_c, y_d), wg.reshape(N_BRANCH, D_MODEL, D_MODEL), b_gate[l],
                       wb.reshape(N_BRANCH, BRANCH_WIDTH, D_MODEL), tm_merge, 512)
        x2 = _outproj(x2, mixed, _cast_bf16(w_out, l, 1024, D_MODEL), tm_merge, 1024)
        x2 = _ffn(x2, norm_ffn_g[l][None], _cast_bf16(ffn_w_up, l, 1024, D_FF // 2), ffn_conv_w[l],
                  ffn_conv_b[l][None], _cast_bf16(ffn_w_down, l, D_FF // 11, D_MODEL), norm_final_g[None],
                  s, tm_ffn, 512, l == DEPTH - 1)
    return x2.reshape(b, s, D_MODEL)
```

```python
import functools
import math

import jax
import jax.numpy as jnp
from jax import lax
from jax.experimental import pallas as pl
from jax.experimental.pallas import tpu as pltpu

F32 = jnp.float32
BF16 = jnp.bfloat16

D_MODEL = 2048
DEPTH = 2
CHUNK = 64
BRANCH_WIDTH = 512
N_BRANCH = 4
GMLP_BLOCK = 128
A_GROUPS = 4
B_HEADS = 8
C_KDIM = 128
D_HEAD = 64
D_FF = 5632
ROPE_THETA = 10000.0
EPS = 1e-6
LANES = 128
LOG2E = 1.4426950408889634

IN_COLS_PAD = 6400
IN_TILE = 1280
COL_A = 0
COL_BQ, COL_BK, COL_BV = 8, 12, 16
COL_CQ, COL_CI, COL_CG = 20, 24, 28
COL_DQ, COL_DK = 32, 36
COL_CZ, COL_BZ, COL_DV = 40, 44, 45
COL32_CZ, COL32_BZ = COL_CZ - 40, COL_BZ - 40

VMEM_LIMIT = 56 * 1024 * 1024


def _cparams(*sem):
    return pltpu.CompilerParams(dimension_semantics=sem, vmem_limit_bytes=VMEM_LIMIT)


def _rms(x, g):
    ms = jnp.mean(x * x, axis=-1, keepdims=True)
    return x * lax.rsqrt(ms + EPS) * g


def _dot(a, b):
    return jnp.dot(a, b, preferred_element_type=F32)


def _dot_nt(a, b):
    return lax.dot_general(a, b, (((1,), (1,)), ((), ())), preferred_element_type=F32)


def _dot_tn(a, b):
    return lax.dot_general(a, b, (((0,), (0,)), ((), ())), preferred_element_type=F32)


def _tri_cumsum(tri, x):
    hi = x.astype(BF16)
    r1 = x - hi.astype(F32)
    mid = r1.astype(BF16)
    lo = (r1 - mid.astype(F32)).astype(BF16)
    return (_dot(tri, hi) + _dot(tri, mid)) + _dot(tri, lo)


def _lower_tri(n):
    return (lax.broadcasted_iota(jnp.int32, (n, n), 0) >= lax.broadcasted_iota(jnp.int32, (n, n), 1)).astype(BF16)


def _cast_kernel(w_ref, o_ref):
    o_ref[...] = w_ref[...].astype(BF16)


def _cast_bf16(w, layer, br, bc):
    _, r, c = w.shape
    return pl.pallas_call(
        _cast_kernel,
        grid=(r // br, c // bc),
        in_specs=[pl.BlockSpec((None, br, bc), lambda i, j: (layer, i, j))],
        out_specs=pl.BlockSpec((br, bc), lambda i, j: (i, j)),
        out_shape=jax.ShapeDtypeStruct((r, c), BF16),
        compiler_params=_cparams("parallel", "parallel"),
        name="cast_bf16",
    )(w)


def _rms_inproj_kernel(x_ref, g_ref, w_ref, p_ref, p32_ref, xn_ref):
    @pl.when(pl.program_id(1) == 0)
    def _():
        xn_ref[...] = _rms(x_ref[...], g_ref[...]).astype(BF16)

    acc = _dot(xn_ref[...], w_ref[...])
    p_ref[...] = acc.astype(BF16)

    @pl.when(pl.program_id(1) == pl.num_programs(1) - 1)
    def _():
        p32_ref[...] = acc


def _rms_inproj(x2, g, w, tm, tn):
    t, np_ = x2.shape[0], w.shape[1]
    return pl.pallas_call(
        _rms_inproj_kernel,
        grid=(t // tm, np_ // tn),
        in_specs=[pl.BlockSpec((tm, D_MODEL), lambda i, j: (i, 0)),
                  pl.BlockSpec((1, D_MODEL), lambda i, j: (0, 0)),
                  pl.BlockSpec((D_MODEL, tn), lambda i, j: (0, j))],
        out_specs=[pl.BlockSpec((tm, tn), lambda i, j: (i, j)),
                   pl.BlockSpec((tm, tn), lambda i, j: (i, 0)),
                   pl.BlockSpec((tm, D_MODEL), lambda i, j: (i, 0))],
        out_shape=[jax.ShapeDtypeStruct((t, np_), BF16), jax.ShapeDtypeStruct((t, tn), F32),
                   jax.ShapeDtypeStruct((t, D_MODEL), BF16)],
        compiler_params=_cparams("parallel", "arbitrary"),
        name="rms_inproj",
    )(x2, g, w)


def _gmlp_kernel(p_ref, lng_ref, lnb_ref, ws_ref, bs_ref, y_ref):
    ta = p_ref.shape[0]
    pa = jax.nn.gelu(p_ref[...].astype(F32))
    u = pa[:, :BRANCH_WIDTH]
    v = pa[:, BRANCH_WIDTH:]
    mu = jnp.mean(v, axis=-1, keepdims=True)
    d = v - mu
    var = jnp.mean(d * d, axis=-1, keepdims=True)
    vn = (d * lax.rsqrt(var + EPS) * lng_ref[...] + lnb_ref[...]).astype(BF16)
    rc = lax.broadcasted_iota(jnp.int32, (GMLP_BLOCK, GMLP_BLOCK), 0) // CHUNK
    cc = lax.broadcasted_iota(jnp.int32, (GMLP_BLOCK, GMLP_BLOCK), 1) // CHUNK
    mask = rc >= cc
    for g in range(A_GROUPS):
        wg = jnp.where(mask, ws_ref[g], 0.0).astype(BF16)
        bcol = bs_ref[:, g:g + 1]
        cs = slice(g * LANES, (g + 1) * LANES)
        for blk in range(ta // GMLP_BLOCK):
            rs = slice(blk * GMLP_BLOCK, (blk + 1) * GMLP_BLOCK)
            mixed = _dot(wg, vn[rs, cs]) + bcol
            y_ref[rs, cs] = (u[rs, cs] * mixed).astype(BF16)


def _gmlp(p, ln_g, ln_b, w_s, b_s_t, ta):
    t = p.shape[0]
    return pl.pallas_call(
        _gmlp_kernel,
        grid=(t // ta,),
        in_specs=[pl.BlockSpec((ta, 2 * BRANCH_WIDTH), lambda i: (i, 0)),
                  pl.BlockSpec((1, BRANCH_WIDTH), lambda i: (0, 0)),
                  pl.BlockSpec((1, BRANCH_WIDTH), lambda i: (0, 0)),
                  pl.BlockSpec((A_GROUPS, GMLP_BLOCK, GMLP_BLOCK), lambda i: (0, 0, 0)),
                  pl.BlockSpec((GMLP_BLOCK, A_GROUPS), lambda i: (0, 0))],
        out_specs=pl.BlockSpec((ta, BRANCH_WIDTH), lambda i: (i, 0)),
        out_shape=jax.ShapeDtypeStruct((t, BRANCH_WIDTH), BF16),
        compiler_params=_cparams("parallel"),
        name="gmlp",
    )(p, ln_g, ln_b, w_s, b_s_t)


def _foxgate_kernel(p_ref, bf_ref, c_ref, ct_ref, *, tb):
    s = p_ref.shape[0]
    seg = 256 if s % 256 == 0 else s
    z = p_ref[...] + bf_ref[...]
    ls = jnp.minimum(z, 0.0) - jnp.log1p(jnp.exp(-jnp.abs(z)))
    tri = _lower_tri(seg)
    carry = jnp.zeros((1, LANES), F32)
    for blk in range(s // seg):
        rs = slice(blk * seg, (blk + 1) * seg)
        cs = _tri_cumsum(tri, ls[rs]) + carry
        c_ref[rs, :] = cs * LOG2E
        carry = cs[seg - 1:seg, :]
    ct = c_ref[...].T
    for j in range(s // tb):
        ct_ref[0, j] = ct[:B_HEADS, j * tb:(j + 1) * tb]


def _foxgate(p, bf, b, s, tb):
    return pl.pallas_call(
        functools.partial(_foxgate_kernel, tb=tb),
        grid=(b,),
        in_specs=[pl.BlockSpec((s, LANES), lambda i: (i, COL32_BZ)),
                  pl.BlockSpec((1, LANES), lambda i: (0, 0))],
        out_specs=[pl.BlockSpec((s, LANES), lambda i: (i, 0)),
                   pl.BlockSpec((1, s // tb, B_HEADS, tb), lambda i: (i, 0, 0, 0))],
        out_shape=[jax.ShapeDtypeStruct((b * s, LANES), F32),
                   jax.ShapeDtypeStruct((b, s // tb, B_HEADS, tb), F32)],
        compiler_params=_cparams("parallel"),
        name="foxgate",
    )(p, bf)


def _flash(streams, kt_ref, vb_ref, n_full, tk, diag_mask):
    tq = streams[0][0].shape[0]

    def step(j, carry, masked):
        ktj = kt_ref[j]
        vj = vb_ref[pl.ds(pl.multiple_of(j * tk, tk), tk), :]
        new = []
        scs = [_dot(qh, ktj) for qh, _ in streams]
        for sc, (_, bias_fn), (m, l, acc) in zip(scs, streams, carry):
            if bias_fn is not None:
                sc = bias_fn(sc, j)
            if masked:
                sc = jnp.where(diag_mask, sc, -jnp.inf)
            m_new = jnp.maximum(m, jnp.max(sc, axis=-1, keepdims=True))
            alpha = jnp.exp2(m - m_new)
            pr = jnp.exp2(sc - m_new)
            l = alpha * l + jnp.sum(pr, axis=-1, keepdims=True)
            acc = alpha * acc + _dot(pr.astype(BF16), vj)
            new.append((m_new, l, acc))
        return tuple(new)

    init = tuple((jnp.full((tq, 1), -jnp.inf, F32), jnp.zeros((tq, 1), F32), jnp.zeros((tq, LANES), F32))
                 for _ in streams)
    carry = lax.fori_loop(0, n_full, lambda j, c: step(j, c, False), init)
    return [acc / l for _, l, acc in step(n_full, carry, True)]


def _flash_t(groups, n_full, tk, diag_mask):
    flat = [(gi,) + tuple(st) for gi, (_, _, streams) in enumerate(groups) for st in streams]
    tq = flat[0][1].shape[1]

    def step(j, carry, masked):
        ks = [keys(j) for keys, _, _ in groups]
        vts = [values_t(j) for _, values_t, _ in groups]
        scs = [_dot(ks[gi], qt) for gi, qt, _, _ in flat]
        new = []
        for sc, (gi, _, bias_fn, qbias), (m, l, acc) in zip(scs, flat, carry):
            if bias_fn is not None:
                sc = bias_fn(sc, j)
            if masked:
                sc = jnp.where(diag_mask, sc, -jnp.inf)
            col_max = jnp.max(sc, axis=0, keepdims=True)
            m_new = jnp.maximum(m, col_max if qbias is None else col_max + qbias)
            alpha = jnp.exp2(m - m_new)
            pr = jnp.exp2(sc - (m_new if qbias is None else m_new - qbias))
            l = alpha * l + jnp.sum(pr, axis=0, keepdims=True)
            acc = alpha * acc + _dot(vts[gi], pr.astype(BF16))
            new.append((m_new, l, acc))
        return tuple(new)

    init = tuple((jnp.full((1, tq), -jnp.inf, F32), jnp.zeros((1, tq), F32), jnp.zeros((LANES, tq), F32))
                 for _ in flat)
    carry = lax.fori_loop(0, n_full, lambda j, c: step(j, c, False), init)
    return [acc / l for _, l, acc in step(n_full, carry, True)]


def _diag_offsets(qi, tq, tk):
    jd = (qi * tq) // tk
    return jd, jd * tk - qi * tq


def _fox_kernel(q_ref, k_ref, v_ref, c_ref, ct_ref, o_ref, vt_s, kc_s, *, tq, tk):
    s = q_ref.shape[0]
    hb = q_ref.shape[1] // LANES
    pair0 = pl.program_id(1) * hb
    lane = lax.broadcasted_iota(jnp.int32, (1, LANES), 1)
    for j in range(s // tk):
        rs = slice(j * tk, (j + 1) * tk)
        for hd in range(hb):
            vt_s[hd, j] = v_ref[rs, hd * LANES:(hd + 1) * LANES].astype(F32).T.astype(BF16)
            for hh in range(2):
                head = 2 * (pair0 + hd) + hh
                col = jnp.sum(jnp.where(lane == head, c_ref[rs, :], 0.0), axis=-1, keepdims=True)
                kc_s[2 * hd + hh, rs, :] = jnp.broadcast_to(col, (tk, LANES))
    col_minus_row = lax.broadcasted_iota(jnp.int32, (tk, tq), 1) - lax.broadcasted_iota(jnp.int32, (tk, tq), 0)
    feat = lax.broadcasted_iota(jnp.int32, (LANES, 1), 0)
    scale = (D_HEAD ** -0.5) * LOG2E

    def qblock(qi, _):
        rs = pl.ds(pl.multiple_of(qi * tq, tq), tq)
        jd, off = _diag_offsets(qi, tq, tk)
        groups = []
        for hd in range(hb):
            cs = slice(hd * LANES, (hd + 1) * LANES)
            qt = (q_ref[rs, cs].astype(F32) * scale).T
            streams = []
            for hh in range(2):
                qh = jnp.where((feat // D_HEAD) == hh, qt, 0.0).astype(BF16)
                crow = ct_ref[0, qi, pl.ds(2 * (pair0 + hd) + hh, 1), :]

                def key_bias(sc, j, slot=2 * hd + hh):
                    kc = kc_s[slot, pl.ds(pl.multiple_of(j * tk, tk), tk), :]
                    return sc - jnp.concatenate([kc] * (tq // LANES), axis=1)

                streams.append((qh, key_bias, crow))
            groups.append((lambda j, cs=cs: k_ref[pl.ds(pl.multiple_of(j * tk, tk), tk), cs],
                           lambda j, hd=hd: vt_s[hd, j], streams))
        outs = _flash_t(groups, jd, tk, col_minus_row >= off)
        for hd in range(hb):
            o_ref[rs, hd * LANES:(hd + 1) * LANES] = jnp.where(
                feat < D_HEAD, outs[2 * hd], outs[2 * hd + 1]).T.astype(BF16)
        return 0

    lax.fori_loop(0, s // tq, qblock, 0)


def _fox(p, c, ct, b, s, tq, tk, hb):
    nk = s // tk
    hw = hb * LANES
    return pl.pallas_call(
        functools.partial(_fox_kernel, tq=tq, tk=tk),
        grid=(b, BRANCH_WIDTH // hw),
        in_specs=[pl.BlockSpec((s, hw), lambda i, h: (i, COL_BQ // hb + h)),
                  pl.BlockSpec((s, hw), lambda i, h: (i, COL_BK // hb + h)),
                  pl.BlockSpec((s, hw), lambda i, h: (i, COL_BV // hb + h)),
                  pl.BlockSpec((s, LANES), lambda i, h: (i, 0)),
                  pl.BlockSpec((1, nk, B_HEADS, tk), lambda i, h: (i, 0, 0, 0))],
        out_specs=pl.BlockSpec((s, hw), lambda i, h: (i, h)),
        out_shape=jax.ShapeDtypeStruct((b * s, BRANCH_WIDTH), BF16),
        scratch_shapes=[pltpu.VMEM((hb, nk, LANES, tk), BF16), pltpu.VMEM((2 * hb, s, LANES), F32)],
        compiler_params=_cparams("parallel", "arbitrary"),
        name="fox_attn",
    )(p, p, p, c, ct)


def _diff_kernel(*refs, tq, tk, lam_init, hb):
    q_refs, k_refs, v_refs = refs[:hb], refs[hb:2 * hb], refs[2 * hb:3 * hb]
    cos_ref, sin_ref, lam_ref, g_ref, o_ref, kb_s, vt_s = refs[3 * hb:]
    s = o_ref.shape[0]
    lane = lax.broadcasted_iota(jnp.int32, (1, LANES), 1)
    lower_half = (lane % D_HEAD) < (D_HEAD // 2)

    def rope(x, cos, sin_signed):
        partner = jnp.where(lower_half, pltpu.roll(x, LANES - D_HEAD // 2, 1), pltpu.roll(x, D_HEAD // 2, 1))
        return x * cos + partner * sin_signed

    for j in range(s // tk):
        rs = slice(j * tk, (j + 1) * tk)
        for hd in range(hb):
            kb_s[hd, rs, :] = rope(k_refs[hd][rs, :].astype(F32), cos_ref[rs, :], sin_ref[rs, :]).astype(BF16)
            vt_s[hd, j] = v_refs[hd][rs, :].astype(F32).T.astype(BF16)

    lp = lam_ref[...]
    lam = (jnp.exp(jnp.sum(lp[0:1] * lp[1:2], axis=-1, keepdims=True))
           - jnp.exp(jnp.sum(lp[2:3] * lp[3:4], axis=-1, keepdims=True)) + lam_init)
    chunk_col_minus_row = (lax.broadcasted_iota(jnp.int32, (tk, tq), 1) // CHUNK
                           - lax.broadcasted_iota(jnp.int32, (tk, tq), 0) // CHUNK)
    feat = lax.broadcasted_iota(jnp.int32, (LANES, 1), 0)
    scale = (D_HEAD ** -0.5) * LOG2E

    def qblock(qi, _):
        rs = pl.ds(pl.multiple_of(qi * tq, tq), tq)
        groups = []
        for hd in range(hb):
            qt = (rope(q_refs[hd][rs, :].astype(F32), cos_ref[rs, :], sin_ref[rs, :]) * scale).T
            q1 = jnp.where(feat < D_HEAD, qt, 0.0).astype(BF16)
            q2 = jnp.where(feat >= D_HEAD, qt, 0.0).astype(BF16)
            groups.append((lambda j, hd=hd: kb_s[hd, pl.ds(pl.multiple_of(j * tk, tk), tk), :],
                           lambda j, hd=hd: vt_s[hd, j], [(q1, None, None), (q2, None, None)]))
        jd, off = _diag_offsets(qi, tq, tk)
        outs = _flash_t(groups, jd, tk, chunk_col_minus_row >= off // CHUNK)
        for hd in range(hb):
            o = (outs[2 * hd] - lam * outs[2 * hd + 1]).T
            o_ref[rs, hd * LANES:(hd + 1) * LANES] = (_rms(o, g_ref[...]) * (1.0 - lam_init)).astype(BF16)
        return 0

    lax.fori_loop(0, s // tq, qblock, 0)


def _diff(p, cos, sin_signed, lam_p, norm_g, b, s, tq, tk, lam_init, hb):
    nb = s // tk

    def col(base, hd):
        return pl.BlockSpec((s, LANES), lambda i, h: (i, base + hb * h + hd))

    whole = pl.BlockSpec((s, LANES), lambda i, h: (0, 0))
    return pl.pallas_call(
        functools.partial(_diff_kernel, tq=tq, tk=tk, lam_init=lam_init, hb=hb),
        grid=(b, BRANCH_WIDTH // (hb * LANES)),
        in_specs=([col(COL_DQ, hd) for hd in range(hb)] + [col(COL_DK, hd) for hd in range(hb)]
                  + [col(COL_DV, hd) for hd in range(hb)]
                  + [whole, whole, pl.BlockSpec((4, D_HEAD), lambda i, h: (0, 0)),
                     pl.BlockSpec((1, LANES), lambda i, h: (0, 0))]),
        out_specs=pl.BlockSpec((s, hb * LANES), lambda i, h: (i, h)),
        out_shape=jax.ShapeDtypeStruct((b * s, BRANCH_WIDTH), BF16),
        scratch_shapes=[pltpu.VMEM((hb, s, LANES), BF16), pltpu.VMEM((hb, nb, LANES, tk), BF16)],
        compiler_params=_cparams("parallel", "arbitrary"),
        name="diff_attn",
    )(*([p] * (3 * hb)), cos, sin_signed, lam_p, norm_g)


SUB = 16
HALF = 8


def _hgrn_kernel(q_ref, z_ref, i_ref, g_ref, lbl_ref, ng_ref, o_ref, *, layer):
    s = q_ref.shape[0]
    heads = q_ref.shape[1] // LANES
    lbl = lbl_ref[...]
    e = jnp.exp(lbl - jnp.max(lbl, axis=0, keepdims=True))
    pr = e / jnp.sum(e, axis=0, keepdims=True)
    lb_all = pr[0:1] - pr[0:1]
    for r in range(1, layer + 1):
        lb_all = lb_all + pr[r:r + 1]
    tri = _lower_tri(CHUNK)
    lane_s = lax.broadcasted_iota(jnp.int32, (SUB, CHUNK), 1)
    lane_h = lax.broadcasted_iota(jnp.int32, (HALF, CHUNK), 1)
    row_h = lax.broadcasted_iota(jnp.int32, (HALF, CHUNK), 0)
    qscale = C_KDIM ** -0.5

    def head_chunk(rs, cs, st):
        lb = lb_all[:, cs]
        oml = 1.0 - lb
        z = z_ref[rs, cs]
        f = lb + oml * jax.nn.sigmoid(z)
        kk = oml * jax.nn.sigmoid(-z)
        cum = _tri_cumsum(tri, jnp.log(f)) * LOG2E
        q = q_ref[rs, cs].astype(F32) * qscale
        vb = i_ref[rs, cs].astype(BF16)
        inter = _dot_nt((q * jnp.exp2(cum)).astype(BF16), st.astype(BF16))
        blocks = []
        for bi in range(CHUNK // SUB):
            b0 = bi * SUB
            if bi == 0:
                halves = [jnp.zeros((HALF, CHUNK), F32)] * 2
            else:
                ref_row = cum[b0 - 1:b0]
                qi_ = (q[b0:b0 + SUB] * jnp.exp2(cum[b0:b0 + SUB] - ref_row)).astype(BF16)
                ki_ = (kk * jnp.exp2(jnp.minimum(ref_row - cum, 0.0))).astype(BF16)
                sc = jnp.where(lane_s < b0, _dot_nt(qi_, ki_), 0.0)
                halves = [sc[:HALF], sc[HALF:]]
            for si in range(SUB):
                c_s, k_s = cum[b0 + si:b0 + si + 1], kk[b0 + si:b0 + si + 1]
                for hf in range(si // HALF, 2):
                    r0 = b0 + hf * HALF
                    x = q[r0:r0 + HALF] * jnp.exp2(cum[r0:r0 + HALF] - c_s) * k_s
                    col = jnp.sum(x, axis=-1, keepdims=True)
                    halves[hf] = jnp.where(lane_h == b0 + si, col, halves[hf])
            for hf in range(2):
                blocks.append(jnp.where(lane_h <= row_h + (b0 + hf * HALF), halves[hf], 0.0))
        scores = jnp.concatenate(blocks, axis=0)
        out = inter + _dot(scores.astype(BF16), vb)
        last = cum[CHUNK - 1:CHUNK]
        kdec = (kk * jnp.exp2(last - cum)).astype(BF16)
        st_new = st * jnp.exp2(last) + _dot_tn(vb, kdec)
        y = _rms(out, ng_ref[...]) * jax.nn.sigmoid(g_ref[rs, cs].astype(F32))
        o_ref[rs, cs] = y.astype(BF16)
        return st_new

    def chunk_body(n, states):
        rs = pl.ds(pl.multiple_of(n * CHUNK, CHUNK), CHUNK)
        return tuple(head_chunk(rs, slice(hd * LANES, (hd + 1) * LANES), st) for hd, st in enumerate(states))

    lax.fori_loop(0, s // CHUNK, chunk_body, tuple(jnp.zeros((LANES, C_KDIM), F32) for _ in range(heads)))


def _hgrn(p, p32, lb_logits, norm_g, b, s, layer, heads):
    hw = heads * LANES
    return pl.pallas_call(
        functools.partial(_hgrn_kernel, layer=layer),
        grid=(b, BRANCH_WIDTH // hw),
        in_specs=[pl.BlockSpec((s, hw), lambda i, h: (i, COL_CQ // heads + h)),
                  pl.BlockSpec((s, hw), lambda i, h: (i, COL32_CZ // heads + h)),
                  pl.BlockSpec((s, hw), lambda i, h: (i, COL_CI // heads + h)),
                  pl.BlockSpec((s, hw), lambda i, h: (i, COL_CG // heads + h)),
                  pl.BlockSpec((DEPTH, hw), lambda i, h: (0, h)),
                  pl.BlockSpec((1, LANES), lambda i, h: (0, 0))],
        out_specs=pl.BlockSpec((s, hw), lambda i, h: (i, h)),
        out_shape=jax.ShapeDtypeStruct((b * s, BRANCH_WIDTH), BF16),
        compiler_params=_cparams("parallel", "arbitrary"),
        name="hgrn",
    )(p, p32, p, p, lb_logits, norm_g)


def _merge_kernel(xn_ref, ya_ref, yb_ref, yc_ref, yd_ref, wg_ref, bg_ref, wb_ref, o_ref):
    xn = xn_ref[...]
    acc = None
    for n, y_ref in enumerate((ya_ref, yb_ref, yc_ref, yd_ref)):
        gate = jax.nn.sigmoid(_dot(xn, wg_ref[n]) + bg_ref[n:n + 1, :])
        term = gate * _dot(y_ref[...], wb_ref[n])
        acc = term if acc is None else acc + term
    o_ref[...] = acc.astype(BF16)


def _merge(xn, ys, wg, bg, wb, tm, tn):
    t = xn.shape[0]
    yspec = pl.BlockSpec((tm, BRANCH_WIDTH), lambda i, j: (i, 0))
    return pl.pallas_call(
        _merge_kernel,
        grid=(t // tm, D_MODEL // tn),
        in_specs=[pl.BlockSpec((tm, D_MODEL), lambda i, j: (i, 0)), yspec, yspec, yspec, yspec,
                  pl.BlockSpec((N_BRANCH, D_MODEL, tn), lambda i, j: (0, 0, j)),
                  pl.BlockSpec((N_BRANCH, tn), lambda i, j: (0, j)),
                  pl.BlockSpec((N_BRANCH, BRANCH_WIDTH, tn), lambda i, j: (0, 0, j))],
        out_specs=pl.BlockSpec((tm, tn), lambda i, j: (i, j)),
        out_shape=jax.ShapeDtypeStruct((t, D_MODEL), BF16),
        compiler_params=_cparams("parallel", "arbitrary"),
        name="merge",
    )(xn, *ys, wg, bg, wb)


def _outproj_kernel(x_ref, m_ref, w_ref, o_ref):
    o_ref[...] = x_ref[...] + _dot(m_ref[...], w_ref[...])


def _outproj(x2, mixed, w, tm, tn):
    t = x2.shape[0]
    return pl.pallas_call(
        _outproj_kernel,
        grid=(t // tm, D_MODEL // tn),
        in_specs=[pl.BlockSpec((tm, tn), lambda i, j: (i, j)),
                  pl.BlockSpec((tm, D_MODEL), lambda i, j: (i, 0)),
                  pl.BlockSpec((D_MODEL, tn), lambda i, j: (0, j))],
        out_specs=pl.BlockSpec((tm, tn), lambda i, j: (i, j)),
        out_shape=jax.ShapeDtypeStruct((t, D_MODEL), F32),
        compiler_params=_cparams("parallel", "arbitrary"),
        name="outproj",
    )(x2, mixed, w)


CARRY_ROWS = 8


def _ffn_kernel(x_ref, g_ref, wa_ref, wg_ref, cwa_ref, cwg_ref, cba_ref, cbg_ref, wd_ref, gf_ref, o_ref,
                xn_s, carry_s, *, tiles_per_seq, final_norm):
    i = pl.program_id(0)
    j = pl.program_id(1)
    tm, tf = xn_s.shape[0], wa_ref.shape[1]

    @pl.when((i == 0) & (j == 0))
    def _():
        carry_s[...] = jnp.zeros_like(carry_s)

    @pl.when(j == 0)
    def _():
        x = x_ref[...]
        xn_s[...] = _rms(x, g_ref[...]).astype(BF16)
        o_ref[...] = x

    seq_start = (i % tiles_per_seq) == 0
    r8 = lax.broadcasted_iota(jnp.int32, (CARRY_ROWS, tf), 0)

    def conv(h, cw_ref, cb_ref, slot):
        prev = jnp.where(seq_start, 0.0, carry_s[slot])
        carry_s[slot] = h[tm - CARRY_ROWS:tm]
        h1 = pltpu.roll(h, 1, 0)
        h2 = pltpu.roll(h, 2, 0)
        p1 = pltpu.roll(prev, 1, 0)
        p2 = pltpu.roll(prev, 2, 0)
        h1 = jnp.concatenate([jnp.where(r8 < 1, p1, h1[:CARRY_ROWS]), h1[CARRY_ROWS:]], axis=0)
        h2 = jnp.concatenate([jnp.where(r8 < 2, p2, h2[:CARRY_ROWS]), h2[CARRY_ROWS:]], axis=0)
        cw = cw_ref[...]
        return cw[0:1] * h2 + cw[1:2] * h1 + cw[2:3] * h + cb_ref[...]

    xn = xn_s[...]
    ha = conv(_dot(xn, wa_ref[...]), cwa_ref, cba_ref, 2 * j)
    hg = conv(_dot(xn, wg_ref[...]), cwg_ref, cbg_ref, 2 * j + 1)
    act = (jax.nn.gelu(ha) * hg).astype(BF16)
    o_ref[...] += _dot(act, wd_ref[...])

    if final_norm:
        @pl.when(j == pl.num_programs(1) - 1)
        def _():
            o_ref[...] = _rms(o_ref[...], gf_ref[...])


def _ffn(x2, g, w_up, conv_w, conv_b, w_down, g_final, s, tm, tf, final_norm):
    t = x2.shape[0]
    nj = D_FF // tf
    return pl.pallas_call(
        functools.partial(_ffn_kernel, tiles_per_seq=s // tm, final_norm=final_norm),
        grid=(t // tm, nj),
        in_specs=[pl.BlockSpec((tm, D_MODEL), lambda i, j: (i, 0), pipeline_mode=pl.Buffered(1)),
                  pl.BlockSpec((1, D_MODEL), lambda i, j: (0, 0)),
                  pl.BlockSpec((D_MODEL, tf), lambda i, j: (0, j)),
                  pl.BlockSpec((D_MODEL, tf), lambda i, j: (0, j + nj)),
                  pl.BlockSpec((3, tf), lambda i, j: (0, j)),
                  pl.BlockSpec((3, tf), lambda i, j: (0, j + nj)),
                  pl.BlockSpec((1, tf), lambda i, j: (0, j)),
                  pl.BlockSpec((1, tf), lambda i, j: (0, j + nj)),
                  pl.BlockSpec((tf, D_MODEL), lambda i, j: (j, 0)),
                  pl.BlockSpec((1, D_MODEL), lambda i, j: (0, 0))],
        out_specs=pl.BlockSpec((tm, D_MODEL), lambda i, j: (i, 0)),
        out_shape=jax.ShapeDtypeStruct((t, D_MODEL), F32),
        scratch_shapes=[pltpu.VMEM((tm, D_MODEL), BF16), pltpu.VMEM((2 * nj, CARRY_ROWS, tf), F32)],
        compiler_params=_cparams("arbitrary", "arbitrary"),
        name="ffn",
    )(x2, g, w_up, w_up, conv_w, conv_w, conv_b, conv_b, w_down, g_final)


def _pad_w_in(w):
    bw = BRANCH_WIDTH
    a, bqkv, bz, cq, cz, ci, cg, dqk, dv = jnp.split(
        w, [2 * bw, 5 * bw, 5 * bw + B_HEADS, 6 * bw + B_HEADS, 7 * bw + B_HEADS, 8 * bw + B_HEADS,
            9 * bw + B_HEADS, 11 * bw + B_HEADS], axis=1)
    z1 = jnp.zeros((D_MODEL, LANES - B_HEADS), w.dtype)
    z2 = jnp.zeros((D_MODEL, LANES), w.dtype)
    out = jnp.concatenate([a, bqkv, cq, ci, cg, dqk, cz, bz, z1, dv, z2], axis=1).astype(BF16)
    assert out.shape[1] == IN_COLS_PAD
    return out


def _rope_tables(s):
    half = D_HEAD // 2
    inv_freq = ROPE_THETA ** (-jnp.arange(half, dtype=F32) / half)
    ang = jnp.arange(s, dtype=jnp.int32).astype(F32)[:, None] * inv_freq[None, :]
    cos, sin = jnp.cos(ang), jnp.sin(ang)
    cos_full = jnp.tile(cos, (1, LANES // half))
    sin_signed = jnp.tile(jnp.concatenate([-sin, sin], axis=1), (1, LANES // D_HEAD))
    return cos_full, sin_signed


def _tiles(b, s):
    return {"attn_q": min(512, s), "attn_k": min(512, s), "inproj": min(1024, b * s), "merge": min(1024, b * s),
            "ffn": min(1024, s), "hgrn_heads": 4, "diff_heads": 4, "fox_pairs": 4}


def kernel(x, norm_mix_g, w_in, fox_b_f, gmlp_ln_g, gmlp_ln_b, gmlp_w_s, gmlp_b_s, hgrn_lb_logits, hgrn_norm_g, diff_lambda, diff_norm_g, w_branch, w_gate, b_gate, w_out, norm_ffn_g, ffn_w_up, ffn_conv_w, ffn_conv_b, ffn_w_down, norm_final_g):
    b, s, _ = x.shape
    t = b * s
    assert s % GMLP_BLOCK == 0
    tl = _tiles(b, s)
    tq, tk, tm_in, tm_merge, tm_ffn = tl["attn_q"], tl["attn_k"], tl["inproj"], tl["merge"], tl["ffn"]
    assert tq == tk and tq % CHUNK == 0
    x2 = x.reshape(t, D_MODEL)
    cos_full, sin_signed = _rope_tables(s)
    bf_pad = jnp.pad(fox_b_f, ((0, 0), (0, LANES - B_HEADS)))
    for l in range(DEPTH):
        lam_init = 0.8 - 0.6 * math.exp(-0.3 * l)
        p, p32, xn = _rms_inproj(x2, norm_mix_g[l][None], _pad_w_in(w_in[l]), tm_in, IN_TILE)
        y_a = _gmlp(p, gmlp_ln_g[l][None], gmlp_ln_b[l][None], gmlp_w_s[l], gmlp_b_s[l].T, min(256, s))
        c, ct = _foxgate(p32, bf_pad[l][None], b, s, tk)
        y_b = _fox(p, c, ct, b, s, tq, tk, tl["fox_pairs"])
        y_c = _hgrn(p, p32, hgrn_lb_logits, hgrn_norm_g[l][None], b, s, l, tl["hgrn_heads"])
        y_d = _diff(p, cos_full, sin_signed, diff_lambda[l], diff_norm_g[l][None], b, s, tq, tk, lam_init,
                    tl["diff_heads"])
        wg = _cast_bf16(w_gate.reshape(DEPTH, N_BRANCH * D_MODEL, D_MODEL), l, 1024, D_MODEL)
        wb = _cast_bf16(w_branch.reshape(DEPTH, N_BRANCH * BRANCH_WIDTH, D_MODEL), l, 1024, D_MODEL)
        mixed = _merge(xn, (y_a, y_b, y_c, y_d), wg.reshape(N_BRANCH, D_MODEL, D_MODEL), b_gate[l],
                       wb.reshape(N_BRANCH, BRANCH_WIDTH, D_MODEL), tm_merge, 512)
        x2 = _outproj(x2, mixed, _cast_bf16(w_out, l, 1024, D_MODEL), tm_merge, 1024)
        x2 = _ffn(x2, norm_ffn_g[l][None], _cast_bf16(ffn_w_up, l, 1024, D_FF // 2), ffn_conv_w[l],
                  ffn_conv_b[l][None], _cast_bf16(ffn_w_down, l, D_FF // 11, D_MODEL), norm_final_g[None],
                  s, tm_ffn, 512, l == DEPTH - 1)
    return x2.reshape(b, s, D_MODEL)
```

```python
import functools
import math

import jax
import jax.numpy as jnp
from jax import lax
from jax.experimental import pallas as pl
from jax.experimental.pallas import tpu as pltpu

F32 = jnp.float32
BF16 = jnp.bfloat16

D_MODEL = 2048
DEPTH = 2
CHUNK = 64
BRANCH_WIDTH = 512
N_BRANCH = 4
GMLP_BLOCK = 128
A_GROUPS = 4
B_HEADS = 8
C_KDIM = 128
D_HEAD = 64
D_FF = 5632
CONV_W = 3
ROPE_THETA = 10000.0
EPS = 1e-6
LANES = 128
LOG2E = 1.4426950408889634

IN_COLS_PAD = 6400
IN_TILE = 1280
COL_A = 0
COL_BQ, COL_BK, COL_BV = 8, 12, 16
COL_CQ, COL_CI, COL_CG = 20, 24, 28
COL_DQ, COL_DK = 32, 36
COL_CZ, COL_BZ, COL_DV = 40, 44, 45
COL32_CZ, COL32_BZ = COL_CZ - 40, COL_BZ - 40

VMEM_LIMIT = 56 * 1024 * 1024


def _cparams(*sem):
    return pltpu.CompilerParams(dimension_semantics=sem, vmem_limit_bytes=VMEM_LIMIT)


def _rms(x, g):
    ms = jnp.mean(x * x, axis=-1, keepdims=True)
    return x * lax.rsqrt(ms + EPS) * g


def _dot(a, b):
    return jnp.dot(a, b, preferred_element_type=F32)


def _dot_nt(a, b):
    return lax.dot_general(a, b, (((1,), (1,)), ((), ())), preferred_element_type=F32)


def _dot_tn(a, b):
    return lax.dot_general(a, b, (((0,), (0,)), ((), ())), preferred_element_type=F32)


def _tri_cumsum(tri, x):
    hi = x.astype(BF16)
    r1 = x - hi.astype(F32)
    mid = r1.astype(BF16)
    lo = (r1 - mid.astype(F32)).astype(BF16)
    return (_dot(tri, hi) + _dot(tri, mid)) + _dot(tri, lo)


def _lower_tri(n):
    return (lax.broadcasted_iota(jnp.int32, (n, n), 0) >= lax.broadcasted_iota(jnp.int32, (n, n), 1)).astype(BF16)


def _cast_kernel(w_ref, o_ref):
    o_ref[...] = w_ref[...].astype(BF16)


def _cast_bf16(w, layer, br, bc):
    _, r, c = w.shape
    return pl.pallas_call(
        _cast_kernel,
        grid=(r // br, c // bc),
        in_specs=[pl.BlockSpec((None, br, bc), lambda i, j: (layer, i, j))],
        out_specs=pl.BlockSpec((br, bc), lambda i, j: (i, j)),
        out_shape=jax.ShapeDtypeStruct((r, c), BF16),
        compiler_params=_cparams("parallel", "parallel"),
        name="cast_bf16",
    )(w)


def _rms_inproj_kernel(x_ref, g_ref, w_ref, p_ref, p32_ref, xn_ref):
    @pl.when(pl.program_id(1) == 0)
    def _():
        xn_ref[...] = _rms(x_ref[...], g_ref[...]).astype(BF16)

    acc = _dot(xn_ref[...], w_ref[...])
    p_ref[...] = acc.astype(BF16)

    @pl.when(pl.program_id(1) == pl.num_programs(1) - 1)
    def _():
        p32_ref[...] = acc


def _rms_inproj(x2, g, w, tm, tn):
    t, np_ = x2.shape[0], w.shape[1]
    return pl.pallas_call(
        _rms_inproj_kernel,
        grid=(t // tm, np_ // tn),
        in_specs=[pl.BlockSpec((tm, D_MODEL), lambda i, j: (i, 0)),
                  pl.BlockSpec((1, D_MODEL), lambda i, j: (0, 0)),
                  pl.BlockSpec((D_MODEL, tn), lambda i, j: (0, j))],
        out_specs=[pl.BlockSpec((tm, tn), lambda i, j: (i, j)),
                   pl.BlockSpec((tm, tn), lambda i, j: (i, 0)),
                   pl.BlockSpec((tm, D_MODEL), lambda i, j: (i, 0))],
        out_shape=[jax.ShapeDtypeStruct((t, np_), BF16), jax.ShapeDtypeStruct((t, tn), F32),
                   jax.ShapeDtypeStruct((t, D_MODEL), BF16)],
        compiler_params=_cparams("parallel", "arbitrary"),
        name="rms_inproj",
    )(x2, g, w)


def _gmlp_kernel(p_ref, lng_ref, lnb_ref, ws_ref, bs_ref, y_ref):
    ta = p_ref.shape[0]
    pa = jax.nn.gelu(p_ref[...].astype(F32))
    u = pa[:, :BRANCH_WIDTH]
    v = pa[:, BRANCH_WIDTH:]
    mu = jnp.mean(v, axis=-1, keepdims=True)
    d = v - mu
    var = jnp.mean(d * d, axis=-1, keepdims=True)
    vn = (d * lax.rsqrt(var + EPS) * lng_ref[...] + lnb_ref[...]).astype(BF16)
    rc = lax.broadcasted_iota(jnp.int32, (GMLP_BLOCK, GMLP_BLOCK), 0) // CHUNK
    cc = lax.broadcasted_iota(jnp.int32, (GMLP_BLOCK, GMLP_BLOCK), 1) // CHUNK
    mask = rc >= cc
    for g in range(A_GROUPS):
        wg = jnp.where(mask, ws_ref[g], 0.0).astype(BF16)
        bcol = bs_ref[:, g:g + 1]
        cs = slice(g * LANES, (g + 1) * LANES)
        for blk in range(ta // GMLP_BLOCK):
            rs = slice(blk * GMLP_BLOCK, (blk + 1) * GMLP_BLOCK)
            mixed = _dot(wg, vn[rs, cs]) + bcol
            y_ref[rs, cs] = (u[rs, cs] * mixed).astype(BF16)


def _gmlp(p, ln_g, ln_b, w_s, b_s_t, ta):
    t = p.shape[0]
    return pl.pallas_call(
        _gmlp_kernel,
        grid=(t // ta,),
        in_specs=[pl.BlockSpec((ta, 2 * BRANCH_WIDTH), lambda i: (i, 0)),
                  pl.BlockSpec((1, BRANCH_WIDTH), lambda i: (0, 0)),
                  pl.BlockSpec((1, BRANCH_WIDTH), lambda i: (0, 0)),
                  pl.BlockSpec((A_GROUPS, GMLP_BLOCK, GMLP_BLOCK), lambda i: (0, 0, 0)),
                  pl.BlockSpec((GMLP_BLOCK, A_GROUPS), lambda i: (0, 0))],
        out_specs=pl.BlockSpec((ta, BRANCH_WIDTH), lambda i: (i, 0)),
        out_shape=jax.ShapeDtypeStruct((t, BRANCH_WIDTH), BF16),
        compiler_params=_cparams("parallel"),
        name="gmlp",
    )(p, ln_g, ln_b, w_s, b_s_t)


def _foxgate_kernel(p_ref, bf_ref, c_ref, ct_ref, *, tb):
    s = p_ref.shape[0]
    seg = 256 if s % 256 == 0 else s
    z = p_ref[...] + bf_ref[...]
    ls = jnp.minimum(z, 0.0) - jnp.log1p(jnp.exp(-jnp.abs(z)))
    tri = _lower_tri(seg)
    carry = jnp.zeros((1, LANES), F32)
    for blk in range(s // seg):
        rs = slice(blk * seg, (blk + 1) * seg)
        cs = _tri_cumsum(tri, ls[rs]) + carry
        c_ref[rs, :] = cs * LOG2E
        carry = cs[seg - 1:seg, :]
    ct = c_ref[...].T
    for j in range(s // tb):
        ct_ref[0, j] = ct[:B_HEADS, j * tb:(j + 1) * tb]


def _foxgate(p, bf, b, s, tb):
    return pl.pallas_call(
        functools.partial(_foxgate_kernel, tb=tb),
        grid=(b,),
        in_specs=[pl.BlockSpec((s, LANES), lambda i: (i, COL32_BZ)),
                  pl.BlockSpec((1, LANES), lambda i: (0, 0))],
        out_specs=[pl.BlockSpec((s, LANES), lambda i: (i, 0)),
                   pl.BlockSpec((1, s // tb, B_HEADS, tb), lambda i: (i, 0, 0, 0))],
        out_shape=[jax.ShapeDtypeStruct((b * s, LANES), F32),
                   jax.ShapeDtypeStruct((b, s // tb, B_HEADS, tb), F32)],
        compiler_params=_cparams("parallel"),
        name="foxgate",
    )(p, bf)


def _flash_t(groups, n_full, tk, diag_mask):
    flat = [(gi,) + tuple(st) for gi, (_, _, streams) in enumerate(groups) for st in streams]
    tq = flat[0][1].shape[1]

    def step(j, carry, masked):
        ks = [keys(j) for keys, _, _ in groups]
        vts = [values_t(j) for _, values_t, _ in groups]
        scs = [_dot(ks[gi], qt) for gi, qt, _, _ in flat]
        new = []
        for sc, (gi, _, bias_fn, qbias), (m, l, acc) in zip(scs, flat, carry):
            if bias_fn is not None:
                sc = bias_fn(sc, j)
            if masked:
                sc = jnp.where(diag_mask, sc, -jnp.inf)
            col_max = jnp.max(sc, axis=0, keepdims=True)
            m_new = jnp.maximum(m, col_max if qbias is None else col_max + qbias)
            alpha = jnp.exp2(m - m_new)
            pr = jnp.exp2(sc - (m_new if qbias is None else m_new - qbias))
            l = alpha * l + jnp.sum(pr, axis=0, keepdims=True)
            acc = alpha * acc + _dot(vts[gi], pr.astype(BF16))
            new.append((m_new, l, acc))
        return tuple(new)

    init = tuple((jnp.full((1, tq), -jnp.inf, F32), jnp.zeros((1, tq), F32), jnp.zeros((LANES, tq), F32))
                 for _ in flat)
    carry = lax.fori_loop(0, n_full, lambda j, c: step(j, c, False), init)
    return [acc / l for _, l, acc in step(n_full, carry, True)]


def _diag_offsets(qi, tq, tk):
    jd = (qi * tq) // tk
    return jd, jd * tk - qi * tq


def _fox_kernel(q_ref, k_ref, v_ref, c_ref, ct_ref, o_ref, vt_s, kc_s, *, tq, tk):
    s = q_ref.shape[0]
    hb = q_ref.shape[1] // LANES
    pair0 = pl.program_id(1) * hb
    lane = lax.broadcasted_iota(jnp.int32, (1, LANES), 1)
    for j in range(s // tk):
        rs = slice(j * tk, (j + 1) * tk)
        for hd in range(hb):
            vt_s[hd, j] = v_ref[rs, hd * LANES:(hd + 1) * LANES].astype(F32).T.astype(BF16)
            for hh in range(2):
                head = 2 * (pair0 + hd) + hh
                col = jnp.sum(jnp.where(lane == head, c_ref[rs, :], 0.0), axis=-1, keepdims=True)
                kc_s[2 * hd + hh, rs, :] = jnp.broadcast_to(col, (tk, LANES))
    col_minus_row = lax.broadcasted_iota(jnp.int32, (tk, tq), 1) - lax.broadcasted_iota(jnp.int32, (tk, tq), 0)
    feat = lax.broadcasted_iota(jnp.int32, (LANES, 1), 0)
    scale = (D_HEAD ** -0.5) * LOG2E

    def qblock(qi, _):
        rs = pl.ds(pl.multiple_of(qi * tq, tq), tq)
        jd, off = _diag_offsets(qi, tq, tk)
        groups = []
        for hd in range(hb):
            cs = slice(hd * LANES, (hd + 1) * LANES)
            qt = (q_ref[rs, cs].astype(F32) * scale).T
            streams = []
            for hh in range(2):
                qh = jnp.where((feat // D_HEAD) == hh, qt, 0.0).astype(BF16)
                crow = ct_ref[0, qi, pl.ds(2 * (pair0 + hd) + hh, 1), :]

                def key_bias(sc, j, slot=2 * hd + hh):
                    kc = kc_s[slot, pl.ds(pl.multiple_of(j * tk, tk), tk), :]
                    return sc - jnp.concatenate([kc] * (tq // LANES), axis=1)

                streams.append((qh, key_bias, crow))
            groups.append((lambda j, cs=cs: k_ref[pl.ds(pl.multiple_of(j * tk, tk), tk), cs],
                           lambda j, hd=hd: vt_s[hd, j], streams))
        outs = _flash_t(groups, jd, tk, col_minus_row >= off)
        for hd in range(hb):
            o_ref[rs, hd * LANES:(hd + 1) * LANES] = jnp.where(
                feat < D_HEAD, outs[2 * hd], outs[2 * hd + 1]).T.astype(BF16)
        return 0

    lax.fori_loop(0, s // tq, qblock, 0)


def _fox(p, c, ct, b, s, tq, tk, hb):
    nk = s // tk
    hw = hb * LANES
    return pl.pallas_call(
        functools.partial(_fox_kernel, tq=tq, tk=tk),
        grid=(b, BRANCH_WIDTH // hw),
        in_specs=[pl.BlockSpec((s, hw), lambda i, h: (i, COL_BQ // hb + h)),
                  pl.BlockSpec((s, hw), lambda i, h: (i, COL_BK // hb + h)),
                  pl.BlockSpec((s, hw), lambda i, h: (i, COL_BV // hb + h)),
                  pl.BlockSpec((s, LANES), lambda i, h: (i, 0)),
                  pl.BlockSpec((1, nk, B_HEADS, tk), lambda i, h: (i, 0, 0, 0))],
        out_specs=pl.BlockSpec((s, hw), lambda i, h: (i, h)),
        out_shape=jax.ShapeDtypeStruct((b * s, BRANCH_WIDTH), BF16),
        scratch_shapes=[pltpu.VMEM((hb, nk, LANES, tk), BF16), pltpu.VMEM((2 * hb, s, LANES), F32)],
        compiler_params=_cparams("parallel", "arbitrary"),
        name="fox_attn",
    )(p, p, p, c, ct)


def _diff_kernel(*refs, tq, tk, lam_init, hb):
    q_refs, k_refs, v_refs = refs[:hb], refs[hb:2 * hb], refs[2 * hb:3 * hb]
    cos_ref, sin_ref, lam_ref, g_ref, o_ref, kb_s, vt_s = refs[3 * hb:]
    s = o_ref.shape[0]
    lane = lax.broadcasted_iota(jnp.int32, (1, LANES), 1)
    lower_half = (lane % D_HEAD) < (D_HEAD // 2)

    def rope(x, cos, sin_signed):
        partner = jnp.where(lower_half, pltpu.roll(x, LANES - D_HEAD // 2, 1), pltpu.roll(x, D_HEAD // 2, 1))
        return x * cos + partner * sin_signed

    for j in range(s // tk):
        rs = slice(j * tk, (j + 1) * tk)
        for hd in range(hb):
            kb_s[hd, rs, :] = rope(k_refs[hd][rs, :].astype(F32), cos_ref[rs, :], sin_ref[rs, :]).astype(BF16)
            vt_s[hd, j] = v_refs[hd][rs, :].astype(F32).T.astype(BF16)

    lp = lam_ref[...]
    lam = (jnp.exp(jnp.sum(lp[0:1] * lp[1:2], axis=-1, keepdims=True))
           - jnp.exp(jnp.sum(lp[2:3] * lp[3:4], axis=-1, keepdims=True)) + lam_init)
    chunk_col_minus_row = (lax.broadcasted_iota(jnp.int32, (tk, tq), 1) // CHUNK
                           - lax.broadcasted_iota(jnp.int32, (tk, tq), 0) // CHUNK)
    feat = lax.broadcasted_iota(jnp.int32, (LANES, 1), 0)
    scale = (D_HEAD ** -0.5) * LOG2E

    def qblock(qi, _):
        rs = pl.ds(pl.multiple_of(qi * tq, tq), tq)
        groups = []
        for hd in range(hb):
            qt = (rope(q_refs[hd][rs, :].astype(F32), cos_ref[rs, :], sin_ref[rs, :]) * scale).T
            q1 = jnp.where(feat < D_HEAD, qt, 0.0).astype(BF16)
            q2 = jnp.where(feat >= D_HEAD, qt, 0.0).astype(BF16)
            groups.append((lambda j, hd=hd: kb_s[hd, pl.ds(pl.multiple_of(j * tk, tk), tk), :],
                           lambda j, hd=hd: vt_s[hd, j], [(q1, None, None), (q2, None, None)]))
        jd, off = _diag_offsets(qi, tq, tk)
        outs = _flash_t(groups, jd, tk, chunk_col_minus_row >= off // CHUNK)
        for hd in range(hb):
            o = (outs[2 * hd] - lam * outs[2 * hd + 1]).T
            o_ref[rs, hd * LANES:(hd + 1) * LANES] = (_rms(o, g_ref[...]) * (1.0 - lam_init)).astype(BF16)
        return 0

    lax.fori_loop(0, s // tq, qblock, 0)


def _diff(p, cos, sin_signed, lam_p, norm_g, b, s, tq, tk, lam_init, hb):
    nb = s // tk

    def col(base, hd):
        return pl.BlockSpec((s, LANES), lambda i, h: (i, base + hb * h + hd))

    whole = pl.BlockSpec((s, LANES), lambda i, h: (0, 0))
    return pl.pallas_call(
        functools.partial(_diff_kernel, tq=tq, tk=tk, lam_init=lam_init, hb=hb),
        grid=(b, BRANCH_WIDTH // (hb * LANES)),
        in_specs=([col(COL_DQ, hd) for hd in range(hb)] + [col(COL_DK, hd) for hd in range(hb)]
                  + [col(COL_DV, hd) for hd in range(hb)]
                  + [whole, whole, pl.BlockSpec((4, D_HEAD), lambda i, h: (0, 0)),
                     pl.BlockSpec((1, LANES), lambda i, h: (0, 0))]),
        out_specs=pl.BlockSpec((s, hb * LANES), lambda i, h: (i, h)),
        out_shape=jax.ShapeDtypeStruct((b * s, BRANCH_WIDTH), BF16),
        scratch_shapes=[pltpu.VMEM((hb, s, LANES), BF16), pltpu.VMEM((hb, nb, LANES, tk), BF16)],
        compiler_params=_cparams("parallel", "arbitrary"),
        name="diff_attn",
    )(*([p] * (3 * hb)), cos, sin_signed, lam_p, norm_g)


SUB = 16
HALF = 8


def _hgrn_kernel(q_ref, z_ref, i_ref, g_ref, lbl_ref, ng_ref, o_ref, *, layer, seqs):
    s = q_ref.shape[0] // seqs
    heads = q_ref.shape[1] // LANES
    lbl = lbl_ref[...]
    e = jnp.exp(lbl - jnp.max(lbl, axis=0, keepdims=True))
    pr = e / jnp.sum(e, axis=0, keepdims=True)
    lb_all = pr[0:1] - pr[0:1]
    for r in range(1, layer + 1):
        lb_all = lb_all + pr[r:r + 1]
    tri = _lower_tri(CHUNK)
    lane_s = lax.broadcasted_iota(jnp.int32, (SUB, CHUNK), 1)
    lane_h = lax.broadcasted_iota(jnp.int32, (HALF, CHUNK), 1)
    row_h = lax.broadcasted_iota(jnp.int32, (HALF, CHUNK), 0)
    qscale = C_KDIM ** -0.5

    def head_chunk(rs, cs, st):
        lb = lb_all[:, cs]
        oml = 1.0 - lb
        z = z_ref[rs, cs]
        f = lb + oml * jax.nn.sigmoid(z)
        kk = oml * jax.nn.sigmoid(-z)
        cum = _tri_cumsum(tri, jnp.log(f)) * LOG2E
        cum_k = cum - jnp.log2(kk)
        q = q_ref[rs, cs].astype(F32) * qscale
        vb = i_ref[rs, cs].astype(BF16)
        inter = _dot_nt((q * jnp.exp2(cum)).astype(BF16), st.astype(BF16))
        blocks = []
        for bi in range(CHUNK // SUB):
            b0 = bi * SUB
            if bi == 0:
                halves = [jnp.zeros((HALF, CHUNK), F32)] * 2
            else:
                ref_row = cum[b0 - 1:b0]
                qi_ = (q[b0:b0 + SUB] * jnp.exp2(cum[b0:b0 + SUB] - ref_row)).astype(BF16)
                ki_ = (kk * jnp.exp2(jnp.minimum(ref_row - cum, 0.0))).astype(BF16)
                sc = jnp.where(lane_s < b0, _dot_nt(qi_, ki_), 0.0)
                halves = [sc[:HALF], sc[HALF:]]
            for si in range(SUB):
                d_s = cum_k[b0 + si:b0 + si + 1]
                for hf in range(si // HALF, 2):
                    r0 = b0 + hf * HALF
                    x = q[r0:r0 + HALF] * jnp.exp2(cum[r0:r0 + HALF] - d_s)
                    col = jnp.sum(x, axis=-1, keepdims=True)
                    halves[hf] = jnp.where(lane_h == b0 + si, col, halves[hf])
            for hf in range(2):
                blocks.append(jnp.where(lane_h <= row_h + (b0 + hf * HALF), halves[hf], 0.0))
        scores = jnp.concatenate(blocks, axis=0)
        out = inter + _dot(scores.astype(BF16), vb)
        last = cum[CHUNK - 1:CHUNK]
        kdec = (kk * jnp.exp2(last - cum)).astype(BF16)
        st_new = st * jnp.exp2(last) + _dot_tn(vb, kdec)
        y = _rms(out, ng_ref[...]) * jax.nn.sigmoid(g_ref[rs, cs].astype(F32))
        o_ref[rs, cs] = y.astype(BF16)
        return st_new

    def chunk_body(n, states):
        new = []
        for sq in range(seqs):
            rs = pl.ds(pl.multiple_of(sq * s + n * CHUNK, CHUNK), CHUNK)
            for hd in range(heads):
                new.append(head_chunk(rs, slice(hd * LANES, (hd + 1) * LANES), states[sq * heads + hd]))
        return tuple(new)

    lax.fori_loop(0, s // CHUNK, chunk_body,
                  tuple(jnp.zeros((LANES, C_KDIM), F32) for _ in range(seqs * heads)))


def _hgrn(p, p32, lb_logits, norm_g, b, s, layer, heads, seqs):
    hw = heads * LANES
    rows = seqs * s
    return pl.pallas_call(
        functools.partial(_hgrn_kernel, layer=layer, seqs=seqs),
        grid=(b // seqs, BRANCH_WIDTH // hw),
        in_specs=[pl.BlockSpec((rows, hw), lambda i, h: (i, COL_CQ // heads + h)),
                  pl.BlockSpec((rows, hw), lambda i, h: (i, COL32_CZ // heads + h)),
                  pl.BlockSpec((rows, hw), lambda i, h: (i, COL_CI // heads + h)),
                  pl.BlockSpec((rows, hw), lambda i, h: (i, COL_CG // heads + h)),
                  pl.BlockSpec((DEPTH, hw), lambda i, h: (0, h)),
                  pl.BlockSpec((1, LANES), lambda i, h: (0, 0))],
        out_specs=pl.BlockSpec((rows, hw), lambda i, h: (i, h)),
        out_shape=jax.ShapeDtypeStruct((b * s, BRANCH_WIDTH), BF16),
        compiler_params=_cparams("parallel", "arbitrary"),
        name="hgrn",
    )(p, p32, p, p, lb_logits, norm_g)


def _merge_kernel(xn_ref, ya_ref, yb_ref, yc_ref, yd_ref, wg_ref, bg_ref, wb_ref, o_ref):
    xn = xn_ref[...]
    acc = None
    for n, y_ref in enumerate((ya_ref, yb_ref, yc_ref, yd_ref)):
        gate = jax.nn.sigmoid(_dot(xn, wg_ref[n]) + bg_ref[n:n + 1, :])
        term = gate * _dot(y_ref[...], wb_ref[n])
        acc = term if acc is None else acc + term
    o_ref[...] = acc.astype(BF16)


def _merge(xn, ys, wg, bg, wb, tm, tn):
    t = xn.shape[0]
    yspec = pl.BlockSpec((tm, BRANCH_WIDTH), lambda i, j: (i, 0))
    return pl.pallas_call(
        _merge_kernel,
        grid=(t // tm, D_MODEL // tn),
        in_specs=[pl.BlockSpec((tm, D_MODEL), lambda i, j: (i, 0)), yspec, yspec, yspec, yspec,
                  pl.BlockSpec((N_BRANCH, D_MODEL, tn), lambda i, j: (0, 0, j)),
                  pl.BlockSpec((N_BRANCH, tn), lambda i, j: (0, j)),
                  pl.BlockSpec((N_BRANCH, BRANCH_WIDTH, tn), lambda i, j: (0, 0, j))],
        out_specs=pl.BlockSpec((tm, tn), lambda i, j: (i, j)),
        out_shape=jax.ShapeDtypeStruct((t, D_MODEL), BF16),
        compiler_params=_cparams("parallel", "arbitrary"),
        name="merge",
    )(xn, *ys, wg, bg, wb)


def _outproj_kernel(x_ref, m_ref, w_ref, o_ref):
    o_ref[...] = x_ref[...] + _dot(m_ref[...], w_ref[...])


def _outproj(x2, mixed, w, tm, tn):
    t = x2.shape[0]
    return pl.pallas_call(
        _outproj_kernel,
        grid=(t // tm, D_MODEL // tn),
        in_specs=[pl.BlockSpec((tm, tn), lambda i, j: (i, j)),
                  pl.BlockSpec((tm, D_MODEL), lambda i, j: (i, 0)),
                  pl.BlockSpec((D_MODEL, tn), lambda i, j: (0, j))],
        out_specs=pl.BlockSpec((tm, tn), lambda i, j: (i, j)),
        out_shape=jax.ShapeDtypeStruct((t, D_MODEL), F32),
        compiler_params=_cparams("parallel", "arbitrary"),
        name="outproj",
    )(x2, mixed, w)


CARRY_ROWS = 8


def _ffn_kernel(x_ref, g_ref, wa_ref, wg_ref, cwb_ref, wd_ref, gf_ref, o_ref, xn_s, carry_s, *,
                tiles_per_seq, final_norm):
    i = pl.program_id(0)
    j = pl.program_id(1)
    nj = pl.num_programs(1)
    tm, tf = xn_s.shape[0], wa_ref.shape[1]

    @pl.when((i == 0) & (j == 0))
    def _():
        carry_s[...] = jnp.zeros_like(carry_s)

    @pl.when(j == 0)
    def _():
        x = x_ref[...]
        xn_s[...] = _rms(x, g_ref[...]).astype(BF16)
        o_ref[...] = x

    seq_start = (i % tiles_per_seq) == 0
    r8 = lax.broadcasted_iota(jnp.int32, (CARRY_ROWS, tf), 0)

    def conv(h, slot):
        prev = jnp.where(seq_start, 0.0, carry_s[slot])
        carry_s[slot] = h[tm - CARRY_ROWS:tm]
        h1 = pltpu.roll(h, 1, 0)
        h2 = pltpu.roll(h, 2, 0)
        p1 = pltpu.roll(prev, 1, 0)
        p2 = pltpu.roll(prev, 2, 0)
        h1 = jnp.concatenate([jnp.where(r8 < 1, p1, h1[:CARRY_ROWS]), h1[CARRY_ROWS:]], axis=0)
        h2 = jnp.concatenate([jnp.where(r8 < 2, p2, h2[:CARRY_ROWS]), h2[CARRY_ROWS:]], axis=0)
        cw = cwb_ref[slot]
        return cw[0:1] * h2 + cw[1:2] * h1 + cw[2:3] * h + cw[3:4]

    xn = xn_s[...]
    ha = conv(_dot(xn, wa_ref[...]), j)
    hg = conv(_dot(xn, wg_ref[...]), j + nj)
    act = (jax.nn.gelu(ha) * hg).astype(BF16)
    o_ref[...] += _dot(act, wd_ref[...])

    if final_norm:
        @pl.when(j == nj - 1)
        def _():
            o_ref[...] = _rms(o_ref[...], gf_ref[...])


def _ffn(x2, g, w_up, conv_w, conv_b, w_down, g_final, s, tm, tf, final_norm):
    t = x2.shape[0]
    nj = D_FF // tf
    cwb = jnp.concatenate([conv_w, conv_b], axis=0).reshape(CONV_W + 1, 2 * nj, tf).transpose(1, 0, 2)
    return pl.pallas_call(
        functools.partial(_ffn_kernel, tiles_per_seq=s // tm, final_norm=final_norm),
        grid=(t // tm, nj),
        in_specs=[pl.BlockSpec((tm, D_MODEL), lambda i, j: (i, 0), pipeline_mode=pl.Buffered(1)),
                  pl.BlockSpec((1, D_MODEL), lambda i, j: (0, 0)),
                  pl.BlockSpec((D_MODEL, tf), lambda i, j: (0, j)),
                  pl.BlockSpec((D_MODEL, tf), lambda i, j: (0, j + nj)),
                  pl.BlockSpec((2 * nj, CONV_W + 1, tf), lambda i, j: (0, 0, 0)),
                  pl.BlockSpec((tf, D_MODEL), lambda i, j: (j, 0)),
                  pl.BlockSpec((1, D_MODEL), lambda i, j: (0, 0))],
        out_specs=pl.BlockSpec((tm, D_MODEL), lambda i, j: (i, 0)),
        out_shape=jax.ShapeDtypeStruct((t, D_MODEL), F32),
        scratch_shapes=[pltpu.VMEM((tm, D_MODEL), BF16), pltpu.VMEM((2 * nj, CARRY_ROWS, tf), F32)],
        compiler_params=_cparams("arbitrary", "arbitrary"),
        name="ffn",
    )(x2, g, w_up, w_up, cwb, w_down, g_final)


def _pad_w_in(w):
    bw = BRANCH_WIDTH
    a, bqkv, bz, cq, cz, ci, cg, dqk, dv = jnp.split(
        w, [2 * bw, 5 * bw, 5 * bw + B_HEADS, 6 * bw + B_HEADS, 7 * bw + B_HEADS, 8 * bw + B_HEADS,
            9 * bw + B_HEADS, 11 * bw + B_HEADS], axis=1)
    z1 = jnp.zeros((D_MODEL, LANES - B_HEADS), w.dtype)
    z2 = jnp.zeros((D_MODEL, LANES), w.dtype)
    out = jnp.concatenate([a, bqkv, cq, ci, cg, dqk, cz, bz, z1, dv, z2], axis=1).astype(BF16)
    assert out.shape[1] == IN_COLS_PAD
    return out


def _rope_tables(s):
    half = D_HEAD // 2
    inv_freq = ROPE_THETA ** (-jnp.arange(half, dtype=F32) / half)
    ang = jnp.arange(s, dtype=jnp.int32).astype(F32)[:, None] * inv_freq[None, :]
    cos, sin = jnp.cos(ang), jnp.sin(ang)
    cos_full = jnp.tile(cos, (1, LANES // half))
    sin_signed = jnp.tile(jnp.concatenate([-sin, sin], axis=1), (1, LANES // D_HEAD))
    return cos_full, sin_signed


def _tiles(b, s):
    return {"attn_q": min(512, s), "attn_k": min(512, s), "inproj": min(1024, b * s), "merge": min(1024, b * s),
            "ffn": min(1024, s), "hgrn_heads": 4, "hgrn_seqs": 2 if b % 2 == 0 else 1, "diff_heads": 4, "fox_pairs": 4}


def kernel(x, norm_mix_g, w_in, fox_b_f, gmlp_ln_g, gmlp_ln_b, gmlp_w_s, gmlp_b_s, hgrn_lb_logits, hgrn_norm_g, diff_lambda, diff_norm_g, w_branch, w_gate, b_gate, w_out, norm_ffn_g, ffn_w_up, ffn_conv_w, ffn_conv_b, ffn_w_down, norm_final_g):
    b, s, _ = x.shape
    t = b * s
    assert s % GMLP_BLOCK == 0
    tl = _tiles(b, s)
    tq, tk, tm_in, tm_merge, tm_ffn = tl["attn_q"], tl["attn_k"], tl["inproj"], tl["merge"], tl["ffn"]
    assert tq == tk and tq % CHUNK == 0
    x2 = x.reshape(t, D_MODEL)
    cos_full, sin_signed = _rope_tables(s)
    bf_pad = jnp.pad(fox_b_f, ((0, 0), (0, LANES - B_HEADS)))
    for l in range(DEPTH):
        lam_init = 0.8 - 0.6 * math.exp(-0.3 * l)
        p, p32, xn = _rms_inproj(x2, norm_mix_g[l][None], _pad_w_in(w_in[l]), tm_in, IN_TILE)
        y_a = _gmlp(p, gmlp_ln_g[l][None], gmlp_ln_b[l][None], gmlp_w_s[l], gmlp_b_s[l].T, min(256, s))
        c, ct = _foxgate(p32, bf_pad[l][None], b, s, tk)
        y_b = _fox(p, c, ct, b, s, tq, tk, tl["fox_pairs"])
        y_c = _hgrn(p, p32, hgrn_lb_logits, hgrn_norm_g[l][None], b, s, l, tl["hgrn_heads"], tl["hgrn_seqs"])
        y_d = _diff(p, cos_full, sin_signed, diff_lambda[l], diff_norm_g[l][None], b, s, tq, tk, lam_init,
                    tl["diff_heads"])
        wg = _cast_bf16(w_gate.reshape(DEPTH, N_BRANCH * D_MODEL, D_MODEL), l, 1024, D_MODEL)
        wb = _cast_bf16(w_branch.reshape(DEPTH, N_BRANCH * BRANCH_WIDTH, D_MODEL), l, 1024, D_MODEL)
        mixed = _merge(xn, (y_a, y_b, y_c, y_d), wg.reshape(N_BRANCH, D_MODEL, D_MODEL), b_gate[l],
                       wb.reshape(N_BRANCH, BRANCH_WIDTH, D_MODEL), tm_merge, 512)
        x2 = _outproj(x2, mixed, _cast_bf16(w_out, l, 1024, D_MODEL), tm_merge, 1024)
        x2 = _ffn(x2, norm_ffn_g[l][None], _cast_bf16(ffn_w_up, l, 1024, D_FF // 2), ffn_conv_w[l],
                  ffn_conv_b[l][None], _cast_bf16(ffn_w_down, l, D_FF // 11, D_MODEL), norm_final_g[None],
                  s, tm_ffn, 512, l == DEPTH - 1)
    return x2.reshape(b, s, D_MODEL)
```
